```python
import math
import jax, jax.numpy as jnp
from jax import lax
import numpy as np

D_MODEL = 1024
BATCH = 2
SEQ = 8192
DEPTH = 4

N_MIXERS = 2
N_GLA_LAYERS = (DEPTH + 1) // 2
N_SB_LAYERS = DEPTH // 2
D_FF = 2816
EPS = 1e-6

GLA_HEADS = 4
GLA_DK = D_MODEL // 2
GLA_DV = D_MODEL
GLA_HK = GLA_DK // GLA_HEADS
GLA_HV = GLA_DV // GLA_HEADS
GLA_GATE_RANK = 16
GLA_GATE_TAU = 16.0
GLA_CHUNK = 64
GLA_IN_COLS = GLA_DK + GLA_DK + GLA_DV + GLA_DV + GLA_GATE_RANK

SB_HEADS = 16
SB_HD = D_MODEL // SB_HEADS
SB_QBLOCK = 128
SB_IN_COLS = 3 * D_MODEL

kernel_name = "hybrid_gla_stickbreaking_macaron"


def rmsnorm(x, g):
    xf = x.astype(jnp.float32)
    y = xf * lax.rsqrt(jnp.mean(xf * xf, axis=-1, keepdims=True) + EPS)
    return (y * g.astype(jnp.float32)).astype(x.dtype)


def swiglu(h, w_gate, w_up, w_down):
    return (jax.nn.silu(h @ w_gate) * (h @ w_up)) @ w_down


def gla_mixer(h, w_in, w_gk2, b_gk, o_norm, w_out):
    B, S, _ = h.shape
    nc = S // GLA_CHUNK
    proj = h @ w_in
    q, k, v, g, gk_lr = jnp.split(
        proj, [GLA_DK, 2 * GLA_DK, 2 * GLA_DK + GLA_DV, 2 * GLA_DK + 2 * GLA_DV], axis=-1)
    log_a = jax.nn.log_sigmoid((gk_lr @ w_gk2 + b_gk).astype(jnp.float32)) / GLA_GATE_TAU

    def to_chunks(t, d):
        return t.astype(jnp.float32).reshape(B, nc, GLA_CHUNK, GLA_HEADS, d).transpose(1, 0, 3, 2, 4)

    qc = to_chunks(q, GLA_HK) * (GLA_HK ** -0.5)
    kc = to_chunks(k, GLA_HK)
    vc = to_chunks(v, GLA_HV)
    bc = jnp.cumsum(to_chunks(log_a, GLA_HK), axis=3)
    causal = jnp.tril(jnp.ones((GLA_CHUNK, GLA_CHUNK), dtype=bool))

    def step(state, inp):
        q_t, k_t, v_t, b_t = inp
        o_inter = jnp.einsum('bhtk,bhkv->bhtv', q_t * jnp.exp(b_t), state)
        diff = b_t[:, :, :, None, :] - b_t[:, :, None, :, :]
        decay = jnp.exp(jnp.where(causal[:, :, None], diff, -jnp.inf))
        scores = jnp.einsum('bhtk,bhtsk,bhsk->bhts', q_t, decay, k_t)
        o_intra = jnp.einsum('bhts,bhsv->bhtv', scores, v_t)
        b_last = b_t[:, :, -1, :]
        k_dec = k_t * jnp.exp(b_last[:, :, None, :] - b_t)
        state = jnp.exp(b_last)[..., None] * state + jnp.einsum('bhsk,bhsv->bhkv', k_dec, v_t)
        return state, o_inter + o_intra

    state0 = jnp.zeros((B, GLA_HEADS, GLA_HK, GLA_HV), jnp.float32)
    _, o = lax.scan(step, state0, (qc, kc, vc, bc))
    o = o.transpose(1, 0, 3, 2, 4).reshape(B, S, GLA_HEADS, GLA_HV)
    o = rmsnorm(o, o_norm).reshape(B, S, GLA_DV)
    o = o.astype(h.dtype) * jax.nn.silu(g)
    return o @ w_out


def sb_mixer(h, w_in, w_out):
    B, S, _ = h.shape
    nb = S // SB_QBLOCK
    q, k, v = jnp.split(h @ w_in, 3, axis=-1)

    def heads(t):
        return t.reshape(B, S, SB_HEADS, SB_HD).transpose(0, 2, 1, 3)

    q, k, v = heads(q), heads(k), heads(v)
    q_blocks = q.reshape(B, SB_HEADS, nb, SB_QBLOCK, SB_HD).transpose(2, 0, 1, 3, 4)
    s_pos = jnp.arange(S)
    scale = 1.0 / math.sqrt(SB_HD)

    def block(args):
        idx, q_blk = args
        z = jnp.einsum('bhtd,bhsd->bhts', q_blk, k).astype(jnp.float32) * scale
        t_pos = idx * SB_QBLOCK + jnp.arange(SB_QBLOCK)
        mask = s_pos[None, :] < t_pos[:, None]
        sp = jnp.where(mask, jax.nn.softplus(z), 0.0)
        rem = lax.cumsum(sp, axis=3, reverse=True) - sp
        weights = jnp.where(mask, jnp.exp(jax.nn.log_sigmoid(z) - rem), 0.0)
        return jnp.einsum('bhts,bhsd->bhtd', weights.astype(v.dtype), v)

    o = lax.map(block, (jnp.arange(nb), q_blocks))
    o = o.transpose(1, 0, 3, 2, 4).reshape(B, S, D_MODEL)
    return o @ w_out


def setup_inputs(seed: int = 0) -> dict:
    key = jax.random.key(seed)
    ks = jax.random.split(key, 24)
    f32 = jnp.float32

    def nrm(k, shape, fan_in, gain=1.0):
        return jax.random.normal(k, shape, f32) * (gain * fan_in ** -0.5)

    def gain(k, shape):
        return 1.0 + 0.02 * jax.random.normal(k, shape, f32)

    return {
        "x": jax.random.normal(ks[0], (BATCH, SEQ, D_MODEL), f32),
        "ffn1_norm": gain(ks[1], (DEPTH, D_MODEL)),
        "ffn1_w_gate": nrm(ks[2], (DEPTH, D_MODEL, D_FF), D_MODEL),
        "ffn1_w_up": nrm(ks[3], (DEPTH, D_MODEL, D_FF), D_MODEL),
        "ffn1_w_down": nrm(ks[4], (DEPTH, D_FF, D_MODEL), D_FF),
        "mix_norm": gain(ks[5], (DEPTH, D_MODEL)),
        "ffn2_norm": gain(ks[6], (DEPTH, D_MODEL)),
        "ffn2_w_gate": nrm(ks[7], (DEPTH, D_MODEL, D_FF), D_MODEL),
        "ffn2_w_up": nrm(ks[8], (DEPTH, D_MODEL, D_FF), D_MODEL),
        "ffn2_w_down": nrm(ks[9], (DEPTH, D_FF, D_MODEL), D_FF),
        "gla_w_in": nrm(ks[10], (N_GLA_LAYERS, D_MODEL, GLA_IN_COLS), D_MODEL),
        "gla_w_gk2": nrm(ks[11], (N_GLA_LAYERS, GLA_GATE_RANK, GLA_DK), GLA_GATE_RANK),
        "gla_b_gk": 0.1 * jax.random.normal(ks[12], (N_GLA_LAYERS, GLA_DK), f32),
        "gla_o_norm": gain(ks[13], (N_GLA_LAYERS, GLA_HV)),
        "gla_w_out": nrm(ks[14], (N_GLA_LAYERS, GLA_DV, D_MODEL), GLA_DV),
        "sb_w_in": nrm(ks[15], (N_SB_LAYERS, D_MODEL, SB_IN_COLS), D_MODEL),
        "sb_w_out": nrm(ks[16], (N_SB_LAYERS, D_MODEL, D_MODEL), D_MODEL),
        "final_norm": gain(ks[17], (D_MODEL,)),
    }


def reference(x, ffn1_norm, ffn1_w_gate, ffn1_w_up, ffn1_w_down, mix_norm,
              ffn2_norm, ffn2_w_gate, ffn2_w_up, ffn2_w_down,
              gla_w_in, gla_w_gk2, gla_b_gk, gla_o_norm, gla_w_out,
              sb_w_in, sb_w_out, final_norm):
    for i in range(DEPTH):
        x = x + 0.5 * swiglu(rmsnorm(x, ffn1_norm[i]), ffn1_w_gate[i], ffn1_w_up[i], ffn1_w_down[i])
        h = rmsnorm(x, mix_norm[i])
        j = i // N_MIXERS
        if i % N_MIXERS == 0:
            y = gla_mixer(h, gla_w_in[j], gla_w_gk2[j], gla_b_gk[j], gla_o_norm[j], gla_w_out[j])
        else:
            y = sb_mixer(h, sb_w_in[j], sb_w_out[j])
        x = x + y.astype(x.dtype)
        x = x + 0.5 * swiglu(rmsnorm(x, ffn2_norm[i]), ffn2_w_gate[i], ffn2_w_up[i], ffn2_w_down[i])
    return rmsnorm(x, final_norm)
```

```python
import functools

import numpy as np
import jax
import jax.numpy as jnp
from jax import lax
from jax.experimental import pallas as pl
from jax.experimental.pallas import tpu as pltpu

F32 = jnp.float32
BF16 = jnp.bfloat16

EPS = 1e-6
GLA_HEADS = 4
GLA_GATE_RANK = 16
GLA_GATE_TAU = 16.0
GLA_CHUNK = 64
GLA_LEVELS = 6
SB_HEADS = 16
LANES = 128

_NT = (((1,), (1,)), ((), ()))
_TN = (((0,), (0,)), ((), ()))


def _rms(x, g):
    ms = jnp.mean(x * x, axis=-1, keepdims=True)
    return x * lax.rsqrt(ms + EPS) * g


def _dot(a, b):
    return jnp.dot(a, b, preferred_element_type=F32)


def _ffn_kernel(x_ref, g_ref, wg_ref, wu_ref, wd_ref, fg_ref, o_ref, xn_ref, acc_ref, *, final):
    f = pl.program_id(1)

    @pl.when(f == 0)
    def _():
        xn_ref[...] = _rms(x_ref[...], g_ref[...]).astype(BF16)
        acc_ref[...] = jnp.zeros_like(acc_ref)

    xn = xn_ref[...]
    a = _dot(xn, wg_ref[...])
    u = _dot(xn, wu_ref[...])
    h = (a * jax.nn.sigmoid(a)) * u
    acc_ref[...] += _dot(h.astype(BF16), wd_ref[...])

    @pl.when(f == pl.num_programs(1) - 1)
    def _():
        y = x_ref[...] + 0.5 * acc_ref[...]
        if final:
            y = _rms(y, fg_ref[...])
        o_ref[...] = y


def _ffn(x, g, wg, wu, wd, fg, *, final, tm=1024, tf=256):
    m, d = x.shape
    ff = wg.shape[1]
    return pl.pallas_call(
        functools.partial(_ffn_kernel, final=final),
        out_shape=jax.ShapeDtypeStruct((m, d), F32),
        grid=(m // tm, ff // tf),
        in_specs=[
            pl.BlockSpec((tm, d), lambda i, f: (i, 0)),
            pl.BlockSpec((1, d), lambda i, f: (0, 0)),
            pl.BlockSpec((d, tf), lambda i, f: (0, f)),
            pl.BlockSpec((d, tf), lambda i, f: (0, f)),
            pl.BlockSpec((tf, d), lambda i, f: (f, 0)),
            pl.BlockSpec((1, d), lambda i, f: (0, 0)),
        ],
        out_specs=pl.BlockSpec((tm, d), lambda i, f: (i, 0)),
        scratch_shapes=[pltpu.VMEM((tm, d), BF16), pltpu.VMEM((tm, d), F32)],
        compiler_params=pltpu.CompilerParams(dimension_semantics=("parallel", "arbitrary")),
        name="ffn",
    )(x, g, wg, wu, wd, fg)


def _proj_kernel(x_ref, g_ref, w_ref, o_ref, xn_ref):
    @pl.when(pl.program_id(1) == 0)
    def _():
        xn_ref[...] = _rms(x_ref[...], g_ref[...]).astype(BF16)

    o_ref[...] = _dot(xn_ref[...], w_ref[...]).astype(o_ref.dtype)


def _proj(x, g, w, *, tm=512, tn=1024):
    m, d = x.shape
    n = w.shape[1]
    return pl.pallas_call(
        _proj_kernel,
        out_shape=jax.ShapeDtypeStruct((m, n), BF16),
        grid=(m // tm, n // tn),
        in_specs=[
            pl.BlockSpec((tm, d), lambda i, j: (i, 0)),
            pl.BlockSpec((1, d), lambda i, j: (0, 0)),
            pl.BlockSpec((d, tn), lambda i, j: (0, j)),
        ],
        out_specs=pl.BlockSpec((tm, tn), lambda i, j: (i, j)),
        scratch_shapes=[pltpu.VMEM((tm, d), BF16)],
        compiler_params=pltpu.CompilerParams(dimension_semantics=("parallel", "arbitrary")),
        name="sb_proj",
    )(x, g, w)


def _gla_proj_kernel(x_ref, g_ref, w_ref, wlr_ref, wgk_ref, bgk_ref, o_ref, la_ref, xn_ref):
    @pl.when(pl.program_id(1) == 0)
    def _():
        xn = _rms(x_ref[...], g_ref[...]).astype(BF16)
        xn_ref[...] = xn
        lr = _dot(xn, wlr_ref[...])
        z = _dot(lr.astype(BF16), wgk_ref[...]) + bgk_ref[...]
        logsig = jnp.minimum(z, 0.0) - jnp.log1p(jnp.exp(-jnp.abs(z)))
        la_ref[...] = logsig / GLA_GATE_TAU

    o_ref[...] = _dot(xn_ref[...], w_ref[...]).astype(o_ref.dtype)


def _gla_proj(x, g, w, wlr, wgk, bgk, *, tm=512, tn=1024):
    m, d = x.shape
    n = w.shape[1]
    dk = wgk.shape[1]
    return pl.pallas_call(
        _gla_proj_kernel,
        out_shape=(jax.ShapeDtypeStruct((m, n), BF16), jax.ShapeDtypeStruct((m, dk), F32)),
        grid=(m // tm, n // tn),
        in_specs=[
            pl.BlockSpec((tm, d), lambda i, j: (i, 0)),
            pl.BlockSpec((1, d), lambda i, j: (0, 0)),
            pl.BlockSpec((d, tn), lambda i, j: (0, j)),
            pl.BlockSpec((d, LANES), lambda i, j: (0, 0)),
            pl.BlockSpec((LANES, dk), lambda i, j: (0, 0)),
            pl.BlockSpec((1, dk), lambda i, j: (0, 0)),
        ],
        out_specs=(
            pl.BlockSpec((tm, tn), lambda i, j: (i, j)),
            pl.BlockSpec((tm, dk), lambda i, j: (i, 0)),
        ),
        scratch_shapes=[pltpu.VMEM((tm, d), BF16)],
        compiler_params=pltpu.CompilerParams(dimension_semantics=("parallel", "arbitrary")),
        name="gla_proj",
    )(x, g, w, wlr, wgk, bgk)


def _gla_consts():
    c = GLA_CHUNK
    r = np.arange(c)
    tmat = [(r[None, :] <= r[:, None])]
    masks = []
    for lvl in range(GLA_LEVELS):
        n = c >> lvl
        half = n // 2
        mid = (r // n) * n + half
        tmat.append(r[None, :] <= mid[:, None])
        same = (r[:, None] // n) == (r[None, :] // n)
        masks.append(same & ((r[:, None] % n) >= half) & ((r[None, :] % n) < half))
    masks.append(r[:, None] == r[None, :])
    return (np.concatenate(tmat, 0).astype(np.float32),
            np.stack(masks, 0).astype(np.float32))


def _gla_kernel(q_ref, k_ref, v_ref, g_ref, la_ref, tm_ref, mask_ref, on_ref, o_ref, st_ref,
                *, nchunk, qscale):
    c = GLA_CHUNK

    @pl.when(pl.program_id(2) == 0)
    def _():
        st_ref[...] = jnp.zeros_like(st_ref)

    tmat = tm_ref[...]
    onorm = on_ref[...]

    def chunk(ci, carry):
        sl = pl.ds(pl.multiple_of(ci * c, c), c)
        la = la_ref[sl, :]
        h1 = la.astype(BF16)
        r1 = la - h1.astype(F32)
        h2 = r1.astype(BF16)
        h3 = (r1 - h2.astype(F32)).astype(BF16)
        bm = _dot(tmat, h1) + _dot(tmat, h2) + _dot(tmat, h3)
        b = bm[0:c]
        q = q_ref[sl, :].astype(F32) * qscale
        k = k_ref[sl, :].astype(F32)
        v = v_ref[sl, :]
        st = st_ref[...]

        o = lax.dot_general((q * jnp.exp(b)).astype(BF16), st.astype(BF16), _NT,
                            preferred_element_type=F32)
        s = mask_ref[GLA_LEVELS] * lax.dot_general(
            q.astype(BF16), k.astype(BF16), _NT, preferred_element_type=F32)
        for lvl in range(GLA_LEVELS):
            ref = bm[(lvl + 1) * c:(lvl + 2) * c]
            ql = (q * jnp.exp(jnp.minimum(b - ref, 0.0))).astype(BF16)
            kl = (k * jnp.exp(jnp.minimum(ref - b, 0.0))).astype(BF16)
            s = s + mask_ref[lvl] * lax.dot_general(ql, kl, _NT, preferred_element_type=F32)
        o = o + _dot(s.astype(BF16), v)

        blast = b[c - 1:c, :]
        kd = (k * jnp.exp(blast - b)).astype(BF16)
        st_ref[...] = jnp.exp(blast) * st + lax.dot_general(
            v, kd, _TN, preferred_element_type=F32)

        g = g_ref[sl, :].astype(F32)
        y = _rms(o, onorm) * (g * jax.nn.sigmoid(g))
        o_ref[sl, :] = y.astype(o_ref.dtype)
        return carry

    lax.fori_loop(0, nchunk, chunk, 0)


def _gla_mix(proj, la, onorm, batch, seq, *, ts=512):
    m = proj.shape[0]
    dk = la.shape[1]
    hk = dk // GLA_HEADS
    dv = (proj.shape[1] - 2 * dk) // 2
    hv = dv // GLA_HEADS
    nsb = seq // ts
    tmat, masks = _gla_consts()
    kcol = dk // hk
    vcol = (2 * dk) // hv
    gcol = (2 * dk + dv) // hv
    row = lambda b, h, s: b * nsb + s
    return pl.pallas_call(
        functools.partial(_gla_kernel, nchunk=ts // GLA_CHUNK, qscale=float(hk) ** -0.5),
        out_shape=jax.ShapeDtypeStruct((m, dv), BF16),
        grid=(batch, GLA_HEADS, nsb),
        in_specs=[
            pl.BlockSpec((ts, hk), lambda b, h, s: (row(b, h, s), h)),
            pl.BlockSpec((ts, hk), lambda b, h, s: (row(b, h, s), kcol + h)),
            pl.BlockSpec((ts, hv), lambda b, h, s: (row(b, h, s), vcol + h)),
            pl.BlockSpec((ts, hv), lambda b, h, s: (row(b, h, s), gcol + h)),
            pl.BlockSpec((ts, hk), lambda b, h, s: (row(b, h, s), h)),
            pl.BlockSpec(tmat.shape, lambda b, h, s: (0, 0)),
            pl.BlockSpec(masks.shape, lambda b, h, s: (0, 0, 0)),
            pl.BlockSpec((1, hv), lambda b, h, s: (0, 0)),
        ],
        out_specs=pl.BlockSpec((ts, hv), lambda b, h, s: (row(b, h, s), h)),
        scratch_shapes=[pltpu.VMEM((hv, hk), F32)],
        compiler_params=pltpu.CompilerParams(
            dimension_semantics=("parallel", "parallel", "arbitrary")),
        name="gla_mix",
    )(proj, proj, proj, proj, la, jnp.asarray(tmat, BF16), jnp.asarray(masks), onorm)


def _sb_kernel(q_ref, k_ref, v_ref, o_ref, acc_ref, *, tq, hd, scale):
    i = pl.program_id(2)
    q = q_ref[...]
    lane = lax.broadcasted_iota(jnp.int32, q.shape, 1)
    qh = (jnp.where(lane < hd, q, jnp.zeros_like(q)), jnp.where(lane >= hd, q, jnp.zeros_like(q)))
    rr = lax.broadcasted_iota(jnp.int32, (tq, tq), 0)
    cc = lax.broadcasted_iota(jnp.int32, (tq, tq), 1)
    suffix = (rr > cc).astype(BF16)
    causal = cc < rr

    def tile(qm, kt, vt, carry, mask):
        z = lax.dot_general(qm, kt, _NT, preferred_element_type=F32) * scale
        l = jnp.log1p(jnp.exp(-jnp.abs(z)))
        sp = jnp.maximum(z, 0.0) + l
        logsig = jnp.minimum(z, 0.0) - l
        if mask is not None:
            sp = jnp.where(mask, sp, 0.0)
        hi = sp.astype(BF16)
        lo = (sp - hi.astype(F32)).astype(BF16)
        rem = _dot(hi, suffix) + _dot(lo, suffix)
        w = jnp.exp(logsig - rem - carry)
        if mask is not None:
            w = jnp.where(mask, w, 0.0)
        pv = _dot(w.astype(BF16), vt)
        return pv, carry + jnp.sum(sp, axis=-1, keepdims=True)

    def sweep(h):
        start = pl.multiple_of(i * tq, tq)
        pv, carry = tile(qh[h], k_ref[pl.ds(start, tq), :], v_ref[pl.ds(start, tq), :],
                         jnp.zeros((tq, 1), F32), causal)
        acc_ref[h] = pv

        def body(t, carry):
            start = pl.multiple_of((i - 1 - t) * tq, tq)
            pv, carry = tile(qh[h], k_ref[pl.ds(start, tq), :], v_ref[pl.ds(start, tq), :],
                             carry, None)
            acc_ref[h] += pv
            return carry

        lax.fori_loop(0, i, body, carry)

    sweep(0)
    sweep(1)
    o_ref[...] = jnp.where(lane < hd, acc_ref[0], acc_ref[1]).astype(o_ref.dtype)


def _sb_mix(qkv, batch, seq, *, tq=256):
    m = qkv.shape[0]
    d = qkv.shape[1] // 3
    hd = d // SB_HEADS
    npair = d // LANES
    nq = seq // tq
    return pl.pallas_call(
        functools.partial(_sb_kernel, tq=tq, hd=hd, scale=float(hd) ** -0.5),
        out_shape=jax.ShapeDtypeStruct((m, d), BF16),
        grid=(batch, npair, nq),
        in_specs=[
            pl.BlockSpec((tq, LANES), lambda b, p, i: (b * nq + i, p)),
            pl.BlockSpec((seq, LANES), lambda b, p, i: (b, npair + p)),
            pl.BlockSpec((seq, LANES), lambda b, p, i: (b, 2 * npair + p)),
        ],
        out_specs=pl.BlockSpec((tq, LANES), lambda b, p, i: (b * nq + i, p)),
        scratch_shapes=[pltpu.VMEM((2, tq, LANES), F32)],
        compiler_params=pltpu.CompilerParams(
            dimension_semantics=("parallel", "parallel", "arbitrary")),
        name="sb_mix",
    )(qkv, qkv, qkv)


def _out_kernel(x_ref, y_ref, w_ref, o_ref):
    o_ref[...] = x_ref[...] + _dot(y_ref[...], w_ref[...])


def _out_proj(x, y, w, *, tm=512):
    m, d = x.shape
    return pl.pallas_call(
        _out_kernel,
        out_shape=jax.ShapeDtypeStruct((m, d), F32),
        grid=(m // tm,),
        in_specs=[
            pl.BlockSpec((tm, d), lambda i: (i, 0)),
            pl.BlockSpec((tm, y.shape[1]), lambda i: (i, 0)),
            pl.BlockSpec(w.shape, lambda i: (0, 0)),
        ],
        out_specs=pl.BlockSpec((tm, d), lambda i: (i, 0)),
        compiler_params=pltpu.CompilerParams(dimension_semantics=("parallel",)),
        name="out_proj",
    )(x, y, w)


def kernel(x, ffn1_norm, ffn1_w_gate, ffn1_w_up, ffn1_w_down, mix_norm, ffn2_norm, ffn2_w_gate, ffn2_w_up, ffn2_w_down, gla_w_in, gla_w_gk2, gla_b_gk, gla_o_norm, gla_w_out, sb_w_in, sb_w_out, final_norm):
    batch, seq, d = x.shape
    depth = ffn1_norm.shape[0]
    dk = gla_w_gk2.shape[2]
    n_main = gla_w_in.shape[2] - GLA_GATE_RANK
    xs = x.reshape(batch * seq, d)
    fg = final_norm.reshape(1, d)
    for i in range(depth):
        xs = _ffn(xs, ffn1_norm[i].reshape(1, d), ffn1_w_gate[i].astype(BF16),
                  ffn1_w_up[i].astype(BF16), ffn1_w_down[i].astype(BF16), fg, final=False)
        j = i // 2
        mg = mix_norm[i].reshape(1, d)
        if i % 2 == 0:
            w_in = gla_w_in[j]
            wlr = jnp.pad(w_in[:, n_main:], ((0, 0), (0, LANES - GLA_GATE_RANK))).astype(BF16)
            wgk = jnp.pad(gla_w_gk2[j], ((0, LANES - GLA_GATE_RANK), (0, 0))).astype(BF16)
            proj, la = _gla_proj(xs, mg, w_in[:, :n_main].astype(BF16), wlr, wgk,
                                 gla_b_gk[j].reshape(1, dk))
            y = _gla_mix(proj, la, gla_o_norm[j].reshape(1, -1), batch, seq)
            xs = _out_proj(xs, y, gla_w_out[j].astype(BF16))
        else:
            qkv = _proj(xs, mg, sb_w_in[j].astype(BF16))
            y = _sb_mix(qkv, batch, seq)
            xs = _out_proj(xs, y, sb_w_out[j].astype(BF16))
        xs = _ffn(xs, ffn2_norm[i].reshape(1, d), ffn2_w_gate[i].astype(BF16),
                  ffn2_w_up[i].astype(BF16), ffn2_w_down[i].astype(BF16), fg,
                  final=(i == depth - 1))
    return xs.reshape(batch, seq, d)
```

```python
import functools

import numpy as np
import jax
import jax.numpy as jnp
from jax import lax
from jax.experimental import pallas as pl
from jax.experimental.pallas import tpu as pltpu

F32 = jnp.float32
BF16 = jnp.bfloat16

EPS = 1e-6
GLA_HEADS = 4
GLA_GATE_RANK = 16
GLA_GATE_TAU = 16.0
GLA_CHUNK = 64
GLA_LEVELS = 6
SB_HEADS = 16
SB_DEAD_MASS = 105.0
LANES = 128

_NT = (((1,), (1,)), ((), ()))
_TN = (((0,), (0,)), ((), ()))


def _rms(x, g):
    ms = jnp.mean(x * x, axis=-1, keepdims=True)
    return x * lax.rsqrt(ms + EPS) * g


def _dot(a, b):
    return jnp.dot(a, b, preferred_element_type=F32)


def _ffn_kernel(x_ref, g_ref, wg_ref, wu_ref, wd_ref, fg_ref, o_ref, xn_ref, acc_ref, *, final):
    f = pl.program_id(1)

    @pl.when(f == 0)
    def _():
        xn_ref[...] = _rms(x_ref[...], g_ref[...]).astype(BF16)
        acc_ref[...] = jnp.zeros_like(acc_ref)

    xn = xn_ref[...]
    a = _dot(xn, wg_ref[...])
    u = _dot(xn, wu_ref[...])
    h = (a * jax.nn.sigmoid(a)) * u
    acc_ref[...] += _dot(h.astype(BF16), wd_ref[...])

    @pl.when(f == pl.num_programs(1) - 1)
    def _():
        y = x_ref[...] + 0.5 * acc_ref[...]
        if final:
            y = _rms(y, fg_ref[...])
        o_ref[...] = y


def _ffn(x, g, wg, wu, wd, fg, *, final, tm=1024, tf=256):
    m, d = x.shape
    ff = wg.shape[1]
    return pl.pallas_call(
        functools.partial(_ffn_kernel, final=final),
        out_shape=jax.ShapeDtypeStruct((m, d), F32),
        grid=(m // tm, ff // tf),
        in_specs=[
            pl.BlockSpec((tm, d), lambda i, f: (i, 0)),
            pl.BlockSpec((1, d), lambda i, f: (0, 0)),
            pl.BlockSpec((d, tf), lambda i, f: (0, f)),
            pl.BlockSpec((d, tf), lambda i, f: (0, f)),
            pl.BlockSpec((tf, d), lambda i, f: (f, 0)),
            pl.BlockSpec((1, d), lambda i, f: (0, 0)),
        ],
        out_specs=pl.BlockSpec((tm, d), lambda i, f: (i, 0)),
        scratch_shapes=[pltpu.VMEM((tm, d), BF16), pltpu.VMEM((tm, d), F32)],
        compiler_params=pltpu.CompilerParams(dimension_semantics=("parallel", "arbitrary")),
        name="ffn",
    )(x, g, wg, wu, wd, fg)


def _proj_kernel(x_ref, g_ref, w_ref, o_ref, xn_ref):
    @pl.when(pl.program_id(1) == 0)
    def _():
        xn_ref[...] = _rms(x_ref[...], g_ref[...]).astype(BF16)

    o_ref[...] = _dot(xn_ref[...], w_ref[...]).astype(o_ref.dtype)


def _proj(x, g, w, *, tm=512, tn=1024):
    m, d = x.shape
    n = w.shape[1]
    return pl.pallas_call(
        _proj_kernel,
        out_shape=jax.ShapeDtypeStruct((m, n), BF16),
        grid=(m // tm, n // tn),
        in_specs=[
            pl.BlockSpec((tm, d), lambda i, j: (i, 0)),
            pl.BlockSpec((1, d), lambda i, j: (0, 0)),
            pl.BlockSpec((d, tn), lambda i, j: (0, j)),
        ],
        out_specs=pl.BlockSpec((tm, tn), lambda i, j: (i, j)),
        scratch_shapes=[pltpu.VMEM((tm, d), BF16)],
        compiler_params=pltpu.CompilerParams(dimension_semantics=("parallel", "arbitrary")),
        name="sb_proj",
    )(x, g, w)


def _gla_proj_kernel(x_ref, g_ref, w_ref, wlr_ref, wgk_ref, bgk_ref, o_ref, la_ref, xn_ref):
    @pl.when(pl.program_id(1) == 0)
    def _():
        xn = _rms(x_ref[...], g_ref[...]).astype(BF16)
        xn_ref[...] = xn
        lr = _dot(xn, wlr_ref[...])
        z = _dot(lr.astype(BF16), wgk_ref[...]) + bgk_ref[...]
        logsig = jnp.minimum(z, 0.0) - jnp.log1p(jnp.exp(-jnp.abs(z)))
        la_ref[...] = logsig / GLA_GATE_TAU

    o_ref[...] = _dot(xn_ref[...], w_ref[...]).astype(o_ref.dtype)


def _gla_proj(x, g, w, wlr, wgk, bgk, *, tm=512, tn=1024):
    m, d = x.shape
    n = w.shape[1]
    dk = wgk.shape[1]
    return pl.pallas_call(
        _gla_proj_kernel,
        out_shape=(jax.ShapeDtypeStruct((m, n), BF16), jax.ShapeDtypeStruct((m, dk), F32)),
        grid=(m // tm, n // tn),
        in_specs=[
            pl.BlockSpec((tm, d), lambda i, j: (i, 0)),
            pl.BlockSpec((1, d), lambda i, j: (0, 0)),
            pl.BlockSpec((d, tn), lambda i, j: (0, j)),
            pl.BlockSpec((d, LANES), lambda i, j: (0, 0)),
            pl.BlockSpec((LANES, dk), lambda i, j: (0, 0)),
            pl.BlockSpec((1, dk), lambda i, j: (0, 0)),
        ],
        out_specs=(
            pl.BlockSpec((tm, tn), lambda i, j: (i, j)),
            pl.BlockSpec((tm, dk), lambda i, j: (i, 0)),
        ),
        scratch_shapes=[pltpu.VMEM((tm, d), BF16)],
        compiler_params=pltpu.CompilerParams(dimension_semantics=("parallel", "arbitrary")),
        name="gla_proj",
    )(x, g, w, wlr, wgk, bgk)


def _gla_consts():
    c = GLA_CHUNK
    r = np.arange(c)
    tmat = [(r[None, :] <= r[:, None])]
    masks = []
    for lvl in range(GLA_LEVELS):
        n = c >> lvl
        half = n // 2
        mid = (r // n) * n + half
        tmat.append(r[None, :] <= mid[:, None])
        same = (r[:, None] // n) == (r[None, :] // n)
        masks.append(same & ((r[:, None] % n) >= half) & ((r[None, :] % n) < half))
    masks.append(r[:, None] == r[None, :])
    return (np.concatenate(tmat, 0).astype(np.float32),
            np.stack(masks, 0).astype(np.float32))


def _gla_kernel(q_ref, k_ref, v_ref, g_ref, la_ref, tm_ref, mask_ref, on_ref, o_ref, st_ref,
                *, nchunk, qscale):
    c = GLA_CHUNK

    @pl.when(pl.program_id(2) == 0)
    def _():
        st_ref[...] = jnp.zeros_like(st_ref)

    tmat = tm_ref[...]
    onorm = on_ref[...]

    def chunk(ci, carry):
        sl = pl.ds(pl.multiple_of(ci * c, c), c)
        la = la_ref[sl, :]
        h1 = la.astype(BF16)
        r1 = la - h1.astype(F32)
        h2 = r1.astype(BF16)
        h3 = (r1 - h2.astype(F32)).astype(BF16)
        bm = _dot(tmat, h1) + _dot(tmat, h2) + _dot(tmat, h3)
        b = bm[0:c]
        q = q_ref[sl, :].astype(F32) * qscale
        k = k_ref[sl, :].astype(F32)
        v = v_ref[sl, :]
        st = st_ref[...]

        o = lax.dot_general((q * jnp.exp(b)).astype(BF16), st.astype(BF16), _NT,
                            preferred_element_type=F32)
        s = mask_ref[GLA_LEVELS] * lax.dot_general(
            q.astype(BF16), k.astype(BF16), _NT, preferred_element_type=F32)
        for lvl in range(GLA_LEVELS):
            ref = bm[(lvl + 1) * c:(lvl + 2) * c]
            ql = (q * jnp.exp(jnp.minimum(b - ref, 0.0))).astype(BF16)
            kl = (k * jnp.exp(jnp.minimum(ref - b, 0.0))).astype(BF16)
            s = s + mask_ref[lvl] * lax.dot_general(ql, kl, _NT, preferred_element_type=F32)
        o = o + _dot(s.astype(BF16), v)

        blast = b[c - 1:c, :]
        kd = (k * jnp.exp(blast - b)).astype(BF16)
        st_ref[...] = jnp.exp(blast) * st + lax.dot_general(
            v, kd, _TN, preferred_element_type=F32)

        g = g_ref[sl, :].astype(F32)
        y = _rms(o, onorm) * (g * jax.nn.sigmoid(g))
        o_ref[sl, :] = y.astype(o_ref.dtype)
        return carry

    lax.fori_loop(0, nchunk, chunk, 0)


def _gla_mix(proj, la, onorm, batch, seq, *, ts=512):
    m = proj.shape[0]
    dk = la.shape[1]
    hk = dk // GLA_HEADS
    dv = (proj.shape[1] - 2 * dk) // 2
    hv = dv // GLA_HEADS
    nsb = seq // ts
    tmat, masks = _gla_consts()
    kcol = dk // hk
    vcol = (2 * dk) // hv
    gcol = (2 * dk + dv) // hv
    row = lambda b, h, s: b * nsb + s
    return pl.pallas_call(
        functools.partial(_gla_kernel, nchunk=ts // GLA_CHUNK, qscale=float(hk) ** -0.5),
        out_shape=jax.ShapeDtypeStruct((m, dv), BF16),
        grid=(batch, GLA_HEADS, nsb),
        in_specs=[
            pl.BlockSpec((ts, hk), lambda b, h, s: (row(b, h, s), h)),
            pl.BlockSpec((ts, hk), lambda b, h, s: (row(b, h, s), kcol + h)),
            pl.BlockSpec((ts, hv), lambda b, h, s: (row(b, h, s), vcol + h)),
            pl.BlockSpec((ts, hv), lambda b, h, s: (row(b, h, s), gcol + h)),
            pl.BlockSpec((ts, hk), lambda b, h, s: (row(b, h, s), h)),
            pl.BlockSpec(tmat.shape, lambda b, h, s: (0, 0)),
            pl.BlockSpec(masks.shape, lambda b, h, s: (0, 0, 0)),
            pl.BlockSpec((1, hv), lambda b, h, s: (0, 0)),
        ],
        out_specs=pl.BlockSpec((ts, hv), lambda b, h, s: (row(b, h, s), h)),
        scratch_shapes=[pltpu.VMEM((hv, hk), F32)],
        compiler_params=pltpu.CompilerParams(
            dimension_semantics=("parallel", "parallel", "arbitrary")),
        name="gla_mix",
    )(proj, proj, proj, proj, la, jnp.asarray(tmat, BF16), jnp.asarray(masks), onorm)


def _sb_kernel(q_ref, k_ref, v_ref, o_ref, acc_ref, *, tq, hd, scale):
    i = pl.program_id(2)
    q = q_ref[...]
    lane = lax.broadcasted_iota(jnp.int32, q.shape, 1)
    qh = (jnp.where(lane < hd, q, jnp.zeros_like(q)), jnp.where(lane >= hd, q, jnp.zeros_like(q)))
    rr = lax.broadcasted_iota(jnp.int32, (tq, tq), 0)
    cc = lax.broadcasted_iota(jnp.int32, (tq, tq), 1)
    suffix = (rr > cc).astype(BF16)
    causal = cc < rr

    def tile(qm, kt, vt, carry, mask):
        z = lax.dot_general(qm, kt, _NT, preferred_element_type=F32) * scale
        l = jnp.log1p(jnp.exp(-jnp.abs(z)))
        sp = jnp.maximum(z, 0.0) + l
        logsig = jnp.minimum(z, 0.0) - l
        if mask is not None:
            sp = jnp.where(mask, sp, 0.0)
        hi = sp.astype(BF16)
        lo = (sp - hi.astype(F32)).astype(BF16)
        rem = _dot(hi, suffix) + _dot(lo, suffix)
        w = jnp.exp(logsig - rem - carry)
        if mask is not None:
            w = jnp.where(mask, w, 0.0)
        pv = _dot(w.astype(BF16), vt)
        return pv, carry + jnp.sum(sp, axis=-1, keepdims=True)

    def sweep(h):
        start = pl.multiple_of(i * tq, tq)
        pv, carry = tile(qh[h], k_ref[pl.ds(start, tq), :], v_ref[pl.ds(start, tq), :],
                         jnp.zeros((tq, 1), F32), causal)
        acc_ref[h] = pv

        def live(c):
            t, _, low = c
            return jnp.logical_and(t < i, low < SB_DEAD_MASS)

        def body(c):
            t, carry, _ = c
            start = pl.multiple_of((i - 1 - t) * tq, tq)
            pv, carry = tile(qh[h], k_ref[pl.ds(start, tq), :], v_ref[pl.ds(start, tq), :],
                             carry, None)
            acc_ref[h] += pv
            return t + 1, carry, jnp.min(carry)

        lax.while_loop(live, body, (jnp.int32(0), carry, jnp.min(carry)))

    sweep(0)
    sweep(1)
    o_ref[...] = jnp.where(lane < hd, acc_ref[0], acc_ref[1]).astype(o_ref.dtype)


def _sb_mix(qkv, batch, seq, *, tq=256):
    m = qkv.shape[0]
    d = qkv.shape[1] // 3
    hd = d // SB_HEADS
    npair = d // LANES
    nq = seq // tq
    return pl.pallas_call(
        functools.partial(_sb_kernel, tq=tq, hd=hd, scale=float(hd) ** -0.5),
        out_shape=jax.ShapeDtypeStruct((m, d), BF16),
        grid=(batch, npair, nq),
        in_specs=[
            pl.BlockSpec((tq, LANES), lambda b, p, i: (b * nq + i, p)),
            pl.BlockSpec((seq, LANES), lambda b, p, i: (b, npair + p)),
            pl.BlockSpec((seq, LANES), lambda b, p, i: (b, 2 * npair + p)),
        ],
        out_specs=pl.BlockSpec((tq, LANES), lambda b, p, i: (b * nq + i, p)),
        scratch_shapes=[pltpu.VMEM((2, tq, LANES), F32)],
        compiler_params=pltpu.CompilerParams(
            dimension_semantics=("parallel", "parallel", "arbitrary")),
        name="sb_mix",
    )(qkv, qkv, qkv)


def _out_kernel(x_ref, y_ref, w_ref, o_ref):
    o_ref[...] = x_ref[...] + _dot(y_ref[...], w_ref[...])


def _out_proj(x, y, w, *, tm=512):
    m, d = x.shape
    return pl.pallas_call(
        _out_kernel,
        out_shape=jax.ShapeDtypeStruct((m, d), F32),
        grid=(m // tm,),
        in_specs=[
            pl.BlockSpec((tm, d), lambda i: (i, 0)),
            pl.BlockSpec((tm, y.shape[1]), lambda i: (i, 0)),
            pl.BlockSpec(w.shape, lambda i: (0, 0)),
        ],
        out_specs=pl.BlockSpec((tm, d), lambda i: (i, 0)),
        compiler_params=pltpu.CompilerParams(dimension_semantics=("parallel",)),
        name="out_proj",
    )(x, y, w)


def kernel(x, ffn1_norm, ffn1_w_gate, ffn1_w_up, ffn1_w_down, mix_norm, ffn2_norm, ffn2_w_gate, ffn2_w_up, ffn2_w_down, gla_w_in, gla_w_gk2, gla_b_gk, gla_o_norm, gla_w_out, sb_w_in, sb_w_out, final_norm):
    batch, seq, d = x.shape
    depth = ffn1_norm.shape[0]
    dk = gla_w_gk2.shape[2]
    n_main = gla_w_in.shape[2] - GLA_GATE_RANK
    xs = x.reshape(batch * seq, d)
    fg = final_norm.reshape(1, d)
    for i in range(depth):
        xs = _ffn(xs, ffn1_norm[i].reshape(1, d), ffn1_w_gate[i].astype(BF16),
                  ffn1_w_up[i].astype(BF16), ffn1_w_down[i].astype(BF16), fg, final=False)
        j = i // 2
        mg = mix_norm[i].reshape(1, d)
        if i % 2 == 0:
            w_in = gla_w_in[j]
            wlr = jnp.pad(w_in[:, n_main:], ((0, 0), (0, LANES - GLA_GATE_RANK))).astype(BF16)
            wgk = jnp.pad(gla_w_gk2[j], ((0, LANES - GLA_GATE_RANK), (0, 0))).astype(BF16)
            proj, la = _gla_proj(xs, mg, w_in[:, :n_main].astype(BF16), wlr, wgk,
                                 gla_b_gk[j].reshape(1, dk))
            y = _gla_mix(proj, la, gla_o_norm[j].reshape(1, -1), batch, seq)
            xs = _out_proj(xs, y, gla_w_out[j].astype(BF16))
        else:
            qkv = _proj(xs, mg, sb_w_in[j].astype(BF16))
            y = _sb_mix(qkv, batch, seq)
            xs = _out_proj(xs, y, sb_w_out[j].astype(BF16))
        xs = _ffn(xs, ffn2_norm[i].reshape(1, d), ffn2_w_gate[i].astype(BF16),
                  ffn2_w_up[i].astype(BF16), ffn2_w_down[i].astype(BF16), fg,
                  final=(i == depth - 1))
    return xs.reshape(batch, seq, d)
```

```python
import functools

import numpy as np
import jax
import jax.numpy as jnp
from jax import lax
from jax.experimental import pallas as pl
from jax.experimental.pallas import tpu as pltpu

F32 = jnp.float32
BF16 = jnp.bfloat16

EPS = 1e-6
GLA_HEADS = 4
GLA_GATE_RANK = 16
GLA_GATE_TAU = 16.0
GLA_CHUNK = 64
GLA_LEVELS = 6
SB_HEADS = 16
LOG2E = 1.4426950408889634
SB_DEAD_MASS_LOG2 = 152.0
LANES = 128

_NT = (((1,), (1,)), ((), ()))
_TN = (((0,), (0,)), ((), ()))


def _rms(x, g):
    ms = jnp.mean(x * x, axis=-1, keepdims=True)
    return x * lax.rsqrt(ms + EPS) * g


def _dot(a, b):
    return jnp.dot(a, b, preferred_element_type=F32)


def _ffn_kernel(x_ref, g_ref, wg_ref, wu_ref, wd_ref, fg_ref, o_ref, xn_ref, acc_ref, *, final):
    f = pl.program_id(1)

    @pl.when(f == 0)
    def _():
        xn_ref[...] = _rms(x_ref[...], g_ref[...]).astype(BF16)
        acc_ref[...] = jnp.zeros_like(acc_ref)

    xn = xn_ref[...]
    a = _dot(xn, wg_ref[...])
    u = _dot(xn, wu_ref[...])
    h = (a * jax.nn.sigmoid(a)) * u
    acc_ref[...] += _dot(h.astype(BF16), wd_ref[...])

    @pl.when(f == pl.num_programs(1) - 1)
    def _():
        y = x_ref[...] + 0.5 * acc_ref[...]
        if final:
            y = _rms(y, fg_ref[...])
        o_ref[...] = y


def _ffn(x, g, wg, wu, wd, fg, *, final, tm=1024, tf=256):
    m, d = x.shape
    ff = wg.shape[1]
    return pl.pallas_call(
        functools.partial(_ffn_kernel, final=final),
        out_shape=jax.ShapeDtypeStruct((m, d), F32),
        grid=(m // tm, ff // tf),
        in_specs=[
            pl.BlockSpec((tm, d), lambda i, f: (i, 0)),
            pl.BlockSpec((1, d), lambda i, f: (0, 0)),
            pl.BlockSpec((d, tf), lambda i, f: (0, f)),
            pl.BlockSpec((d, tf), lambda i, f: (0, f)),
            pl.BlockSpec((tf, d), lambda i, f: (f, 0)),
            pl.BlockSpec((1, d), lambda i, f: (0, 0)),
        ],
        out_specs=pl.BlockSpec((tm, d), lambda i, f: (i, 0)),
        scratch_shapes=[pltpu.VMEM((tm, d), BF16), pltpu.VMEM((tm, d), F32)],
        compiler_params=pltpu.CompilerParams(dimension_semantics=("parallel", "arbitrary")),
        name="ffn",
    )(x, g, wg, wu, wd, fg)


def _project_columns(xn, w_ref, o_ref, tn):
    for c0 in range(0, w_ref.shape[1], tn):
        o_ref[:, c0:c0 + tn] = _dot(xn, w_ref[:, c0:c0 + tn]).astype(o_ref.dtype)


def _proj_kernel(x_ref, g_ref, w_ref, o_ref, *, tn):
    xn = _rms(x_ref[...], g_ref[...]).astype(BF16)
    _project_columns(xn, w_ref, o_ref, tn)


def _proj(x, g, w, *, tm=512, tn=1024):
    m, d = x.shape
    n = w.shape[1]
    return pl.pallas_call(
        functools.partial(_proj_kernel, tn=tn),
        out_shape=jax.ShapeDtypeStruct((m, n), BF16),
        grid=(m // tm,),
        in_specs=[
            pl.BlockSpec((tm, d), lambda i: (i, 0)),
            pl.BlockSpec((1, d), lambda i: (0, 0)),
            pl.BlockSpec((d, n), lambda i: (0, 0)),
        ],
        out_specs=pl.BlockSpec((tm, n), lambda i: (i, 0)),
        compiler_params=pltpu.CompilerParams(dimension_semantics=("parallel",)),
        name="sb_proj",
    )(x, g, w)


def _gla_proj_kernel(x_ref, g_ref, w_ref, wlr_ref, wgk_ref, bgk_ref, o_ref, la_ref, *, tn):
    xn = _rms(x_ref[...], g_ref[...]).astype(BF16)
    lr = _dot(xn, wlr_ref[...])
    z = _dot(lr.astype(BF16), wgk_ref[...]) + bgk_ref[...]
    logsig = jnp.minimum(z, 0.0) - jnp.log1p(jnp.exp(-jnp.abs(z)))
    la_ref[...] = logsig / GLA_GATE_TAU
    _project_columns(xn, w_ref, o_ref, tn)


def _gla_proj(x, g, w, wlr, wgk, bgk, *, tm=512, tn=1024):
    m, d = x.shape
    n = w.shape[1]
    dk = wgk.shape[1]
    return pl.pallas_call(
        functools.partial(_gla_proj_kernel, tn=tn),
        out_shape=(jax.ShapeDtypeStruct((m, n), BF16), jax.ShapeDtypeStruct((m, dk), F32)),
        grid=(m // tm,),
        in_specs=[
            pl.BlockSpec((tm, d), lambda i: (i, 0)),
            pl.BlockSpec((1, d), lambda i: (0, 0)),
            pl.BlockSpec((d, n), lambda i: (0, 0)),
            pl.BlockSpec((d, LANES), lambda i: (0, 0)),
            pl.BlockSpec((LANES, dk), lambda i: (0, 0)),
            pl.BlockSpec((1, dk), lambda i: (0, 0)),
        ],
        out_specs=(
            pl.BlockSpec((tm, n), lambda i: (i, 0)),
            pl.BlockSpec((tm, dk), lambda i: (i, 0)),
        ),
        compiler_params=pltpu.CompilerParams(dimension_semantics=("parallel",)),
        name="gla_proj",
    )(x, g, w, wlr, wgk, bgk)


def _gla_consts():
    c = GLA_CHUNK
    r = np.arange(c)
    tmat = r[None, :] <= r[:, None]
    masks = []
    for lvl in range(GLA_LEVELS):
        n = c >> lvl
        half = n // 2
        same = (r[:, None] // n) == (r[None, :] // n)
        masks.append(same & ((r[:, None] % n) >= half) & ((r[None, :] % n) < half))
    masks.append(r[:, None] == r[None, :])
    return tmat.astype(np.float32), np.stack(masks, 0).astype(np.float32)


def _level_refs(b):
    c, n = b.shape
    sub = 8
    refs = []
    for lvl in range(GLA_LEVELS):
        blk = c >> lvl
        half = blk // 2
        if blk >= 2 * sub:
            refs.append(jnp.concatenate(
                [jnp.broadcast_to(b[m + half:m + half + 1], (blk, n)) for m in range(0, c, blk)], 0))
        else:
            b3 = b.reshape(c // sub, sub, n)
            row = lax.broadcasted_iota(jnp.int32, b3.shape, 1)
            ref = jnp.broadcast_to(b3[:, half:half + 1], b3.shape)
            for m in range(blk, sub, blk):
                ref = jnp.where(row >= m, jnp.broadcast_to(b3[:, m + half:m + half + 1], b3.shape), ref)
            refs.append(ref.reshape(c, n))
    return refs


def _gla_kernel(q_ref, k_ref, v_ref, g_ref, la_ref, tm_ref, mask_ref, on_ref, o_ref, st_ref,
                *, nchunk, qscale):
    c = GLA_CHUNK
    hk = q_ref.shape[1] // GLA_HEADS
    hv = v_ref.shape[1] // GLA_HEADS

    @pl.when(pl.program_id(1) == 0)
    def _():
        st_ref[...] = jnp.zeros_like(st_ref)

    tmat = tm_ref[...]
    onorm = on_ref[...]

    def chunk(ci, carry):
        sl = pl.ds(pl.multiple_of(ci * c, c), c)
        la = la_ref[sl, :]
        h1 = la.astype(BF16)
        r1 = la - h1.astype(F32)
        h2 = r1.astype(BF16)
        h3 = (r1 - h2.astype(F32)).astype(BF16)
        ball = _dot(tmat, h1) + _dot(tmat, h2) + _dot(tmat, h3)
        refs = _level_refs(ball)

        heads = range(GLA_HEADS)
        ksl = [slice(h * hk, (h + 1) * hk) for h in heads]
        vsl = [slice(h * hv, (h + 1) * hv) for h in heads]
        b = [ball[:, ksl[h]] for h in heads]
        q = [q_ref[sl, ksl[h]].astype(F32) * qscale for h in heads]
        k = [k_ref[sl, ksl[h]].astype(F32) for h in heads]
        v = [v_ref[sl, vsl[h]] for h in heads]
        st = [st_ref[h] for h in heads]

        scores = []
        for h in heads:
            s = mask_ref[GLA_LEVELS] * lax.dot_general(
                q[h].astype(BF16), k[h].astype(BF16), _NT, preferred_element_type=F32)
            for lvl in range(GLA_LEVELS):
                ref = refs[lvl][:, ksl[h]]
                f = jnp.exp(-jnp.abs(b[h] - ref))
                s = s + mask_ref[lvl] * lax.dot_general(
                    (q[h] * f).astype(BF16), (k[h] * f).astype(BF16), _NT,
                    preferred_element_type=F32)
            scores.append(s.astype(BF16))

        outs = []
        for h in heads:
            o = lax.dot_general((q[h] * jnp.exp(b[h])).astype(BF16), st[h].astype(BF16), _NT,
                                preferred_element_type=F32)
            outs.append(o + _dot(scores[h], v[h]))

        for h in heads:
            blast = b[h][c - 1:c, :]
            kd = (k[h] * jnp.exp(blast - b[h])).astype(BF16)
            st_ref[h] = jnp.exp(blast) * st[h] + lax.dot_general(
                v[h], kd, _TN, preferred_element_type=F32)

        for h in heads:
            g = g_ref[sl, vsl[h]].astype(F32)
            y = _rms(outs[h], onorm) * (g * jax.nn.sigmoid(g))
            o_ref[sl, vsl[h]] = y.astype(o_ref.dtype)
        return carry

    lax.fori_loop(0, nchunk, chunk, 0, unroll=2)


def _gla_mix(proj, la, onorm, batch, seq, *, ts=512):
    m = proj.shape[0]
    dk = la.shape[1]
    dv = (proj.shape[1] - 2 * dk) // 2
    hk = dk // GLA_HEADS
    hv = dv // GLA_HEADS
    nsb = seq // ts
    tmat, masks = _gla_consts()
    row = lambda b, s: b * nsb + s
    return pl.pallas_call(
        functools.partial(_gla_kernel, nchunk=ts // GLA_CHUNK, qscale=float(hk) ** -0.5),
        out_shape=jax.ShapeDtypeStruct((m, dv), BF16),
        grid=(batch, nsb),
        in_specs=[
            pl.BlockSpec((ts, dk), lambda b, s: (row(b, s), 0)),
            pl.BlockSpec((ts, dk), lambda b, s: (row(b, s), 1)),
            pl.BlockSpec((ts, dv), lambda b, s: (row(b, s), (2 * dk) // dv)),
            pl.BlockSpec((ts, dv), lambda b, s: (row(b, s), (2 * dk) // dv + 1)),
            pl.BlockSpec((ts, dk), lambda b, s: (row(b, s), 0)),
            pl.BlockSpec(tmat.shape, lambda b, s: (0, 0)),
            pl.BlockSpec(masks.shape, lambda b, s: (0, 0, 0)),
            pl.BlockSpec((1, hv), lambda b, s: (0, 0)),
        ],
        out_specs=pl.BlockSpec((ts, dv), lambda b, s: (row(b, s), 0)),
        scratch_shapes=[pltpu.VMEM((GLA_HEADS, hv, hk), F32)],
        compiler_params=pltpu.CompilerParams(dimension_semantics=("parallel", "arbitrary")),
        name="gla_mix",
    )(proj, proj, proj, proj, la, jnp.asarray(tmat, BF16), jnp.asarray(masks), onorm)


def _sb_kernel(q_ref, k_ref, v_ref, o_ref, acc_ref, *, tq, hd, zscale):
    i = pl.program_id(2)
    q = q_ref[...]
    lane = lax.broadcasted_iota(jnp.int32, q.shape, 1)
    qh = (jnp.where(lane < hd, q, jnp.zeros_like(q)), jnp.where(lane >= hd, q, jnp.zeros_like(q)))
    rr = lax.broadcasted_iota(jnp.int32, (2 * tq, tq), 0)
    cc = lax.broadcasted_iota(jnp.int32, (2 * tq, tq), 1)
    suffix = (jnp.where(rr >= tq, rr - tq, rr) >= cc).astype(BF16)
    causal = (lax.broadcasted_iota(jnp.int32, (tq, tq), 1)
              < lax.broadcasted_iota(jnp.int32, (tq, tq), 0))

    def tile(qm, start, carry, mask):
        kt = k_ref[pl.ds(start, tq), :]
        vt = v_ref[pl.ds(start, tq), :]
        z = lax.dot_general(qm, kt, _NT, preferred_element_type=F32) * zscale
        sp = jnp.maximum(z, 0.0) + jnp.log(1.0 + jnp.exp2(-jnp.abs(z))) * LOG2E
        if mask is not None:
            sp = jnp.where(mask, sp, 0.0)
        hi = sp.astype(BF16)
        lo = (sp - hi.astype(F32)).astype(BF16)
        rem = _dot(jnp.concatenate([hi, lo], axis=1), suffix)
        w = jnp.exp2(z - rem - carry)
        if mask is not None:
            w = jnp.where(mask, w, 0.0)
        return _dot(w.astype(BF16), vt), jnp.sum(sp, axis=-1, keepdims=True)

    diag = pl.multiple_of(i * tq, tq)
    prev = pl.multiple_of(jnp.maximum(i - 1, 0) * tq, tq)
    has_prev = (i > 0).astype(F32)
    carry = []
    for h in range(2):
        pv_d, mass_d = tile(qh[h], diag, jnp.zeros((tq, 1), F32), causal)
        pv_p, mass_p = tile(qh[h], prev, mass_d, None)
        acc_ref[h] = pv_d + has_prev * pv_p
        carry.append(mass_d + mass_p)

    def live(c):
        t, _, _, low = c
        return jnp.logical_and(t < i - 1, low < SB_DEAD_MASS_LOG2)

    def body(c):
        t, c0, c1, _ = c
        start = pl.multiple_of((i - 2 - t) * tq, tq)
        pv0, m0 = tile(qh[0], start, c0, None)
        pv1, m1 = tile(qh[1], start, c1, None)
        acc_ref[0] += pv0
        acc_ref[1] += pv1
        c0 = c0 + m0
        c1 = c1 + m1
        return t + 1, c0, c1, jnp.minimum(jnp.min(c0), jnp.min(c1))

    lax.while_loop(live, body, (jnp.int32(0), carry[0], carry[1],
                                jnp.minimum(jnp.min(carry[0]), jnp.min(carry[1]))))
    o_ref[...] = jnp.where(lane < hd, acc_ref[0], acc_ref[1]).astype(o_ref.dtype)


def _sb_mix(qkv, batch, seq, *, tq=256):
    m = qkv.shape[0]
    d = qkv.shape[1] // 3
    hd = d // SB_HEADS
    npair = d // LANES
    nq = seq // tq
    return pl.pallas_call(
        functools.partial(_sb_kernel, tq=tq, hd=hd, zscale=float(hd) ** -0.5 * LOG2E),
        out_shape=jax.ShapeDtypeStruct((m, d), BF16),
        grid=(batch, npair, nq),
        in_specs=[
            pl.BlockSpec((tq, LANES), lambda b, p, i: (b * nq + i, p)),
            pl.BlockSpec((seq, LANES), lambda b, p, i: (b, npair + p)),
            pl.BlockSpec((seq, LANES), lambda b, p, i: (b, 2 * npair + p)),
        ],
        out_specs=pl.BlockSpec((tq, LANES), lambda b, p, i: (b * nq + i, p)),
        scratch_shapes=[pltpu.VMEM((2, tq, LANES), F32)],
        compiler_params=pltpu.CompilerParams(
            dimension_semantics=("parallel", "parallel", "arbitrary")),
        name="sb_mix",
    )(qkv, qkv, qkv)


def _out_kernel(x_ref, y_ref, w_ref, o_ref):
    o_ref[...] = x_ref[...] + _dot(y_ref[...], w_ref[...])


def _out_proj(x, y, w, *, tm=512):
    m, d = x.shape
    return pl.pallas_call(
        _out_kernel,
        out_shape=jax.ShapeDtypeStruct((m, d), F32),
        grid=(m // tm,),
        in_specs=[
            pl.BlockSpec((tm, d), lambda i: (i, 0)),
            pl.BlockSpec((tm, y.shape[1]), lambda i: (i, 0)),
            pl.BlockSpec(w.shape, lambda i: (0, 0)),
        ],
        out_specs=pl.BlockSpec((tm, d), lambda i: (i, 0)),
        compiler_params=pltpu.CompilerParams(dimension_semantics=("parallel",)),
        name="out_proj",
    )(x, y, w)


def kernel(x, ffn1_norm, ffn1_w_gate, ffn1_w_up, ffn1_w_down, mix_norm, ffn2_norm, ffn2_w_gate, ffn2_w_up, ffn2_w_down, gla_w_in, gla_w_gk2, gla_b_gk, gla_o_norm, gla_w_out, sb_w_in, sb_w_out, final_norm):
    batch, seq, d = x.shape
    depth = ffn1_norm.shape[0]
    dk = gla_w_gk2.shape[2]
    n_main = gla_w_in.shape[2] - GLA_GATE_RANK
    xs = x.reshape(batch * seq, d)
    fg = final_norm.reshape(1, d)
    for i in range(depth):
        xs = _ffn(xs, ffn1_norm[i].reshape(1, d), ffn1_w_gate[i].astype(BF16),
                  ffn1_w_up[i].astype(BF16), ffn1_w_down[i].astype(BF16), fg, final=False)
        j = i // 2
        mg = mix_norm[i].reshape(1, d)
        if i % 2 == 0:
            w_in = gla_w_in[j]
            wlr = jnp.pad(w_in[:, n_main:], ((0, 0), (0, LANES - GLA_GATE_RANK))).astype(BF16)
            wgk = jnp.pad(gla_w_gk2[j], ((0, LANES - GLA_GATE_RANK), (0, 0))).astype(BF16)
            proj, la = _gla_proj(xs, mg, w_in[:, :n_main].astype(BF16), wlr, wgk,
                                 gla_b_gk[j].reshape(1, dk))
            y = _gla_mix(proj, la, gla_o_norm[j].reshape(1, -1), batch, seq)
            xs = _out_proj(xs, y, gla_w_out[j].astype(BF16))
        else:
            qkv = _proj(xs, mg, sb_w_in[j].astype(BF16))
            y = _sb_mix(qkv, batch, seq)
            xs = _out_proj(xs, y, sb_w_out[j].astype(BF16))
        xs = _ffn(xs, ffn2_norm[i].reshape(1, d), ffn2_w_gate[i].astype(BF16),
                  ffn2_w_up[i].astype(BF16), ffn2_w_down[i].astype(BF16), fg,
                  final=(i == depth - 1))
    return xs.reshape(batch, seq, d)
```

```python
import functools

import numpy as np
import jax
import jax.numpy as jnp
from jax import lax
from jax.experimental import pallas as pl
from jax.experimental.pallas import tpu as pltpu

F32 = jnp.float32
BF16 = jnp.bfloat16

EPS = 1e-6
GLA_HEADS = 4
GLA_GATE_RANK = 16
GLA_GATE_TAU = 16.0
GLA_CHUNK = 64
GLA_LEVELS = 6
SB_HEADS = 16
LOG2E = 1.4426950408889634
SB_DEAD_MASS_LOG2 = 152.0
LANES = 128
MXU_DIM = 256
FFN_ROWS = 512
FFN_CHUNK = MXU_DIM
PROJ_CHUNK = 4 * MXU_DIM

_NT = (((1,), (1,)), ((), ()))
_TN = (((0,), (0,)), ((), ()))


def _rms(x, g):
    ms = jnp.mean(x * x, axis=-1, keepdims=True)
    return x * lax.rsqrt(ms + EPS) * g


def _dot(a, b):
    return jnp.dot(a, b, preferred_element_type=F32)


def _swiglu_half(x, g_ref, wg_ref, wu_ref, wd_ref):
    xn = _rms(x, g_ref[...]).astype(BF16)
    acc = None
    for f0 in range(0, wg_ref.shape[1], FFN_CHUNK):
        a = _dot(xn, wg_ref[:, f0:f0 + FFN_CHUNK])
        u = _dot(xn, wu_ref[:, f0:f0 + FFN_CHUNK])
        h = (a * jax.nn.sigmoid(a)) * u
        part = _dot(h.astype(BF16), wd_ref[f0:f0 + FFN_CHUNK, :])
        acc = part if acc is None else acc + part
    return x + 0.5 * acc


def _project_columns(xn, w_ref, o_ref):
    for c0 in range(0, w_ref.shape[1], PROJ_CHUNK):
        o_ref[:, c0:c0 + PROJ_CHUNK] = _dot(xn, w_ref[:, c0:c0 + PROJ_CHUNK]).astype(o_ref.dtype)


def _ffn_sb_kernel(x_ref, g_ref, wg_ref, wu_ref, wd_ref, mg_ref, w_ref, o_ref, p_ref):
    y = _swiglu_half(x_ref[...], g_ref, wg_ref, wu_ref, wd_ref)
    o_ref[...] = y
    _project_columns(_rms(y, mg_ref[...]).astype(BF16), w_ref, p_ref)


def _ffn_gla_kernel(x_ref, g_ref, wg_ref, wu_ref, wd_ref, mg_ref, w_ref, wlr_ref, wgk_ref,
                    bgk_ref, o_ref, p_ref, la_ref):
    y = _swiglu_half(x_ref[...], g_ref, wg_ref, wu_ref, wd_ref)
    o_ref[...] = y
    hn = _rms(y, mg_ref[...]).astype(BF16)
    lr = _dot(hn, wlr_ref[...])
    z = _dot(lr.astype(BF16), wgk_ref[...]) + bgk_ref[...]
    logsig = jnp.minimum(z, 0.0) - jnp.log1p(jnp.exp(-jnp.abs(z)))
    la_ref[...] = logsig / GLA_GATE_TAU
    _project_columns(hn, w_ref, p_ref)


def _ffn_out_kernel(x_ref, y_ref, wo_ref, g_ref, wg_ref, wu_ref, wd_ref, fg_ref, o_ref, *, final):
    x = x_ref[...] + _dot(y_ref[...], wo_ref[...])
    y = _swiglu_half(x, g_ref, wg_ref, wu_ref, wd_ref)
    if final:
        y = _rms(y, fg_ref[...])
    o_ref[...] = y


def _resident(a):
    return pl.BlockSpec(a.shape, lambda i: (0,) * a.ndim, pipeline_mode=pl.Buffered(1))


def _rows(tm, n):
    return pl.BlockSpec((tm, n), lambda i: (i, 0))


def _ffn_in(x, ffn_w, mg, mix_w, *, tm=FFN_ROWS):
    m, d = x.shape
    n = mix_w[0].shape[1]
    gla = len(mix_w) > 1
    out_shape = [jax.ShapeDtypeStruct((m, d), F32), jax.ShapeDtypeStruct((m, n), BF16)]
    out_specs = [_rows(tm, d), _rows(tm, n)]
    if gla:
        dk = mix_w[2].shape[1]
        out_shape.append(jax.ShapeDtypeStruct((m, dk), F32))
        out_specs.append(_rows(tm, dk))
    consts = (*ffn_w, mg, *mix_w)
    return pl.pallas_call(
        _ffn_gla_kernel if gla else _ffn_sb_kernel,
        out_shape=tuple(out_shape),
        grid=(m // tm,),
        in_specs=[_rows(tm, d)] + [_resident(a) for a in consts],
        out_specs=tuple(out_specs),
        compiler_params=pltpu.CompilerParams(dimension_semantics=("parallel",)),
        name="ffn_gla_proj" if gla else "ffn_sb_proj",
    )(x, *consts)


def _ffn_out(x, y, wo, ffn_w, fg, *, final, tm=FFN_ROWS):
    m, d = x.shape
    consts = (wo, *ffn_w, fg)
    return pl.pallas_call(
        functools.partial(_ffn_out_kernel, final=final),
        out_shape=jax.ShapeDtypeStruct((m, d), F32),
        grid=(m // tm,),
        in_specs=[_rows(tm, d), _rows(tm, y.shape[1])] + [_resident(a) for a in consts],
        out_specs=_rows(tm, d),
        compiler_params=pltpu.CompilerParams(dimension_semantics=("parallel",)),
        name="out_proj_ffn",
    )(x, y, *consts)


def _gla_consts():
    c = GLA_CHUNK
    r = np.arange(c)
    tmat = r[None, :] <= r[:, None]
    masks = []
    for lvl in range(GLA_LEVELS):
        n = c >> lvl
        half = n // 2
        same = (r[:, None] // n) == (r[None, :] // n)
        masks.append(same & ((r[:, None] % n) >= half) & ((r[None, :] % n) < half))
    masks.append(r[:, None] == r[None, :])
    return tmat.astype(np.float32), np.stack(masks, 0).astype(np.float32)


def _level_refs(b):
    c, n = b.shape
    sub = 8
    refs = []
    for lvl in range(GLA_LEVELS):
        blk = c >> lvl
        half = blk // 2
        if blk >= 2 * sub:
            refs.append(jnp.concatenate(
                [jnp.broadcast_to(b[m + half:m + half + 1], (blk, n)) for m in range(0, c, blk)], 0))
        else:
            b3 = b.reshape(c // sub, sub, n)
            row = lax.broadcasted_iota(jnp.int32, b3.shape, 1)
            ref = jnp.broadcast_to(b3[:, half:half + 1], b3.shape)
            for m in range(blk, sub, blk):
                ref = jnp.where(row >= m, jnp.broadcast_to(b3[:, m + half:m + half + 1], b3.shape), ref)
            refs.append(ref.reshape(c, n))
    return refs


def _gla_kernel(q_ref, k_ref, v_ref, g_ref, la_ref, tm_ref, mask_ref, on_ref, o_ref, st_ref,
                *, nchunk, qscale):
    c = GLA_CHUNK
    hk = q_ref.shape[1] // GLA_HEADS
    hv = v_ref.shape[1] // GLA_HEADS

    @pl.when(pl.program_id(1) == 0)
    def _():
        st_ref[...] = jnp.zeros_like(st_ref)

    tmat = tm_ref[...]
    onorm = on_ref[...]

    def chunk(ci, carry):
        sl = pl.ds(pl.multiple_of(ci * c, c), c)
        la = la_ref[sl, :]
        h1 = la.astype(BF16)
        r1 = la - h1.astype(F32)
        h2 = r1.astype(BF16)
        h3 = (r1 - h2.astype(F32)).astype(BF16)
        ball = _dot(tmat, h1) + _dot(tmat, h2) + _dot(tmat, h3)
        refs = _level_refs(ball)

        heads = range(GLA_HEADS)
        ksl = [slice(h * hk, (h + 1) * hk) for h in heads]
        vsl = [slice(h * hv, (h + 1) * hv) for h in heads]
        b = [ball[:, ksl[h]] for h in heads]
        q = [q_ref[sl, ksl[h]].astype(F32) * qscale for h in heads]
        k = [k_ref[sl, ksl[h]].astype(F32) for h in heads]
        v = [v_ref[sl, vsl[h]] for h in heads]
        st = [st_ref[h] for h in heads]

        scores = []
        for h in heads:
            s = mask_ref[GLA_LEVELS] * lax.dot_general(
                q[h].astype(BF16), k[h].astype(BF16), _NT, preferred_element_type=F32)
            for lvl in range(GLA_LEVELS):
                ref = refs[lvl][:, ksl[h]]
                f = jnp.exp(-jnp.abs(b[h] - ref))
                s = s + mask_ref[lvl] * lax.dot_general(
                    (q[h] * f).astype(BF16), (k[h] * f).astype(BF16), _NT,
                    preferred_element_type=F32)
            scores.append(s.astype(BF16))

        outs = []
        for h in heads:
            o = lax.dot_general((q[h] * jnp.exp(b[h])).astype(BF16), st[h].astype(BF16), _NT,
                                preferred_element_type=F32)
            outs.append(o + _dot(scores[h], v[h]))

        for h in heads:
            blast = b[h][c - 1:c, :]
            kd = (k[h] * jnp.exp(blast - b[h])).astype(BF16)
            st_ref[h] = jnp.exp(blast) * st[h] + lax.dot_general(
                v[h], kd, _TN, preferred_element_type=F32)

        for h in heads:
            g = g_ref[sl, vsl[h]].astype(F32)
            y = _rms(outs[h], onorm) * (g * jax.nn.sigmoid(g))
            o_ref[sl, vsl[h]] = y.astype(o_ref.dtype)
        return carry

    lax.fori_loop(0, nchunk, chunk, 0, unroll=2)


def _gla_mix(proj, la, onorm, batch, seq, *, ts=512):
    m = proj.shape[0]
    dk = la.shape[1]
    dv = (proj.shape[1] - 2 * dk) // 2
    hk = dk // GLA_HEADS
    hv = dv // GLA_HEADS
    nsb = seq // ts
    tmat, masks = _gla_consts()
    row = lambda b, s: b * nsb + s
    return pl.pallas_call(
        functools.partial(_gla_kernel, nchunk=ts // GLA_CHUNK, qscale=float(hk) ** -0.5),
        out_shape=jax.ShapeDtypeStruct((m, dv), BF16),
        grid=(batch, nsb),
        in_specs=[
            pl.BlockSpec((ts, dk), lambda b, s: (row(b, s), 0)),
            pl.BlockSpec((ts, dk), lambda b, s: (row(b, s), 1)),
            pl.BlockSpec((ts, dv), lambda b, s: (row(b, s), (2 * dk) // dv)),
            pl.BlockSpec((ts, dv), lambda b, s: (row(b, s), (2 * dk) // dv + 1)),
            pl.BlockSpec((ts, dk), lambda b, s: (row(b, s), 0)),
            pl.BlockSpec(tmat.shape, lambda b, s: (0, 0)),
            pl.BlockSpec(masks.shape, lambda b, s: (0, 0, 0)),
            pl.BlockSpec((1, hv), lambda b, s: (0, 0)),
        ],
        out_specs=pl.BlockSpec((ts, dv), lambda b, s: (row(b, s), 0)),
        scratch_shapes=[pltpu.VMEM((GLA_HEADS, hv, hk), F32)],
        compiler_params=pltpu.CompilerParams(dimension_semantics=("parallel", "arbitrary")),
        name="gla_mix",
    )(proj, proj, proj, proj, la, jnp.asarray(tmat, BF16), jnp.asarray(masks), onorm)


def _sb_kernel(q_ref, k_ref, v_ref, o_ref, acc_ref, *, tq, hd, zscale):
    i = pl.program_id(2)
    q = q_ref[...]
    lane = lax.broadcasted_iota(jnp.int32, q.shape, 1)
    qh = (jnp.where(lane < hd, q, jnp.zeros_like(q)), jnp.where(lane >= hd, q, jnp.zeros_like(q)))
    rr = lax.broadcasted_iota(jnp.int32, (tq, tq), 0)
    cc = lax.broadcasted_iota(jnp.int32, (tq, tq), 1)
    suffix = (rr >= cc).astype(BF16)
    causal = (lax.broadcasted_iota(jnp.int32, (tq, tq), 1)
              < lax.broadcasted_iota(jnp.int32, (tq, tq), 0))

    def softplus_stage(qm, start, mask):
        z = lax.dot_general(qm, k_ref[pl.ds(start, tq), :], _NT,
                            preferred_element_type=F32) * zscale
        sp = jnp.maximum(z, 0.0) + jnp.log(1.0 + jnp.exp2(-jnp.abs(z))) * LOG2E
        if mask is not None:
            sp = jnp.where(mask, sp, 0.0)
        return z, sp.astype(BF16), jnp.sum(sp, axis=-1, keepdims=True)

    def weight_stage(z, rem, carry, mask):
        w = jnp.exp2(z - rem - carry)
        if mask is not None:
            w = jnp.where(mask, w, 0.0)
        return w.astype(BF16)

    def tiles(jobs, carries):
        first = [softplus_stage(qh[h], start, mask) for h, start, mask in jobs]
        rems = [_dot(pieces, suffix) for _, pieces, _ in first]
        mass = [m for _, _, m in first]
        ws = [weight_stage(first[n][0], rems[n], carries(n, mass), jobs[n][2])
              for n in range(len(jobs))]
        return [_dot(ws[n], v_ref[pl.ds(jobs[n][1], tq), :]) for n in range(len(jobs))], mass

    diag = pl.multiple_of(i * tq, tq)
    prev = pl.multiple_of(jnp.maximum(i - 1, 0) * tq, tq)
    has_prev = (i > 0).astype(F32)
    jobs = [(0, diag, causal), (0, prev, None), (1, diag, causal), (1, prev, None)]
    pv, mass = tiles(jobs, lambda n, mass: mass[n - 1] if n % 2 else jnp.zeros((tq, 1), F32))
    carry = []
    for h in range(2):
        acc_ref[h] = pv[2 * h] + has_prev * pv[2 * h + 1]
        carry.append(mass[2 * h] + mass[2 * h + 1])

    def live(c):
        t, _, _, low = c
        return jnp.logical_and(t < i - 1, low < SB_DEAD_MASS_LOG2)

    def body(c):
        t, c0, c1, _ = c
        start = pl.multiple_of((i - 2 - t) * tq, tq)
        pv, mass = tiles([(0, start, None), (1, start, None)], lambda n, mass: (c0, c1)[n])
        acc_ref[0] += pv[0]
        acc_ref[1] += pv[1]
        c0 = c0 + mass[0]
        c1 = c1 + mass[1]
        return t + 1, c0, c1, jnp.minimum(jnp.min(c0), jnp.min(c1))

    lax.while_loop(live, body, (jnp.int32(0), carry[0], carry[1],
                                jnp.minimum(jnp.min(carry[0]), jnp.min(carry[1]))))
    o_ref[...] = jnp.where(lane < hd, acc_ref[0], acc_ref[1]).astype(o_ref.dtype)


def _sb_mix(qkv, batch, seq, *, tq=256):
    m = qkv.shape[0]
    d = qkv.shape[1] // 3
    hd = d // SB_HEADS
    npair = d // LANES
    nq = seq // tq
    return pl.pallas_call(
        functools.partial(_sb_kernel, tq=tq, hd=hd, zscale=float(hd) ** -0.5 * LOG2E),
        out_shape=jax.ShapeDtypeStruct((m, d), BF16),
        grid=(batch, npair, nq),
        in_specs=[
            pl.BlockSpec((tq, LANES), lambda b, p, i: (b * nq + i, p)),
            pl.BlockSpec((seq, LANES), lambda b, p, i: (b, npair + p)),
            pl.BlockSpec((seq, LANES), lambda b, p, i: (b, 2 * npair + p)),
        ],
        out_specs=pl.BlockSpec((tq, LANES), lambda b, p, i: (b * nq + i, p)),
        scratch_shapes=[pltpu.VMEM((2, tq, LANES), F32)],
        compiler_params=pltpu.CompilerParams(
            dimension_semantics=("parallel", "parallel", "arbitrary")),
        name="sb_mix",
    )(qkv, qkv, qkv)


def kernel(x, ffn1_norm, ffn1_w_gate, ffn1_w_up, ffn1_w_down, mix_norm, ffn2_norm, ffn2_w_gate, ffn2_w_up, ffn2_w_down, gla_w_in, gla_w_gk2, gla_b_gk, gla_o_norm, gla_w_out, sb_w_in, sb_w_out, final_norm):
    batch, seq, d = x.shape
    depth = ffn1_norm.shape[0]
    dk = gla_w_gk2.shape[2]
    n_main = gla_w_in.shape[2] - GLA_GATE_RANK
    xs = x.reshape(batch * seq, d)
    fg = final_norm.reshape(1, d)
    for i in range(depth):
        ffn1 = (ffn1_norm[i].reshape(1, d), ffn1_w_gate[i].astype(BF16),
                ffn1_w_up[i].astype(BF16), ffn1_w_down[i].astype(BF16))
        ffn2 = (ffn2_norm[i].reshape(1, d), ffn2_w_gate[i].astype(BF16),
                ffn2_w_up[i].astype(BF16), ffn2_w_down[i].astype(BF16))
        j = i // 2
        mg = mix_norm[i].reshape(1, d)
        if i % 2 == 0:
            w_in = gla_w_in[j]
            wlr = jnp.pad(w_in[:, n_main:], ((0, 0), (0, LANES - GLA_GATE_RANK))).astype(BF16)
            wgk = jnp.pad(gla_w_gk2[j], ((0, LANES - GLA_GATE_RANK), (0, 0))).astype(BF16)
            xs, proj, la = _ffn_in(xs, ffn1, mg, (w_in[:, :n_main].astype(BF16), wlr, wgk,
                                                  gla_b_gk[j].reshape(1, dk)))
            y = _gla_mix(proj, la, gla_o_norm[j].reshape(1, -1), batch, seq)
            w_out = gla_w_out[j]
        else:
            xs, qkv = _ffn_in(xs, ffn1, mg, (sb_w_in[j].astype(BF16),))
            y = _sb_mix(qkv, batch, seq)
            w_out = sb_w_out[j]
        xs = _ffn_out(xs, y, w_out.astype(BF16), ffn2, fg, final=(i == depth - 1))
    return xs.reshape(batch, seq, d)
```

```python
import functools

import numpy as np
import jax
import jax.numpy as jnp
from jax import lax
from jax.experimental import pallas as pl
from jax.experimental.pallas import tpu as pltpu

F32 = jnp.float32
BF16 = jnp.bfloat16

EPS = 1e-6
GLA_HEADS = 4
GLA_GATE_RANK = 16
GLA_GATE_TAU = 16.0
GLA_CHUNK = 64
GLA_LEVELS = 6
GLA_ROWS = 256
SB_HEADS = 16
SB_PAIRS_PER_STEP = 4
LOG2E = 1.4426950408889634
SB_DEAD_MASS_LOG2 = 152.0
LANES = 128
MXU_DIM = 256
FFN_ROWS = 512
FFN_CHUNK = MXU_DIM
PROJ_CHUNK = 4 * MXU_DIM

_NT = (((1,), (1,)), ((), ()))
_TN = (((0,), (0,)), ((), ()))


def _rms(x, g):
    ms = jnp.mean(x * x, axis=-1, keepdims=True)
    return x * lax.rsqrt(ms + EPS) * g


def _dot(a, b):
    return jnp.dot(a, b, preferred_element_type=F32)


def _swiglu_half(x, g_ref, wg_ref, wu_ref, wd_ref):
    xn = _rms(x, g_ref[...]).astype(BF16)
    acc = None
    for f0 in range(0, wg_ref.shape[1], FFN_CHUNK):
        a = _dot(xn, wg_ref[:, f0:f0 + FFN_CHUNK])
        u = _dot(xn, wu_ref[:, f0:f0 + FFN_CHUNK])
        h = (a * jax.nn.sigmoid(a)) * u
        part = _dot(h.astype(BF16), wd_ref[f0:f0 + FFN_CHUNK, :])
        acc = part if acc is None else acc + part
    return x + 0.5 * acc


def _project_columns(xn, w_ref, o_ref):
    for c0 in range(0, w_ref.shape[1], PROJ_CHUNK):
        o_ref[:, c0:c0 + PROJ_CHUNK] = _dot(xn, w_ref[:, c0:c0 + PROJ_CHUNK]).astype(o_ref.dtype)


def _ffn_sb_kernel(x_ref, g_ref, wg_ref, wu_ref, wd_ref, mg_ref, w_ref, o_ref, p_ref):
    y = _swiglu_half(x_ref[...], g_ref, wg_ref, wu_ref, wd_ref)
    o_ref[...] = y
    _project_columns(_rms(y, mg_ref[...]).astype(BF16), w_ref, p_ref)


def _ffn_gla_kernel(x_ref, g_ref, wg_ref, wu_ref, wd_ref, mg_ref, w_ref, wlr_ref, wgk_ref,
                    bgk_ref, o_ref, p_ref, la_ref):
    y = _swiglu_half(x_ref[...], g_ref, wg_ref, wu_ref, wd_ref)
    o_ref[...] = y
    hn = _rms(y, mg_ref[...]).astype(BF16)
    lr = _dot(hn, wlr_ref[...])
    z = _dot(lr.astype(BF16), wgk_ref[...]) + bgk_ref[...]
    logsig = jnp.minimum(z, 0.0) - jnp.log1p(jnp.exp(-jnp.abs(z)))
    la_ref[...] = logsig / GLA_GATE_TAU
    _project_columns(hn, w_ref, p_ref)


def _ffn_out_kernel(x_ref, y_ref, wo_ref, g_ref, wg_ref, wu_ref, wd_ref, fg_ref, o_ref, *, final):
    x = x_ref[...] + _dot(y_ref[...], wo_ref[...])
    y = _swiglu_half(x, g_ref, wg_ref, wu_ref, wd_ref)
    if final:
        y = _rms(y, fg_ref[...])
    o_ref[...] = y


def _resident(a):
    return pl.BlockSpec(a.shape, lambda i: (0,) * a.ndim, pipeline_mode=pl.Buffered(1))


def _rows(tm, n):
    return pl.BlockSpec((tm, n), lambda i: (i, 0))


def _ffn_in(x, ffn_w, mg, mix_w, *, tm=FFN_ROWS):
    m, d = x.shape
    n = mix_w[0].shape[1]
    gla = len(mix_w) > 1
    out_shape = [jax.ShapeDtypeStruct((m, d), F32), jax.ShapeDtypeStruct((m, n), BF16)]
    out_specs = [_rows(tm, d), _rows(tm, n)]
    if gla:
        dk = mix_w[2].shape[1]
        out_shape.append(jax.ShapeDtypeStruct((m, dk), F32))
        out_specs.append(_rows(tm, dk))
    consts = (*ffn_w, mg, *mix_w)
    return pl.pallas_call(
        _ffn_gla_kernel if gla else _ffn_sb_kernel,
        out_shape=tuple(out_shape),
        grid=(m // tm,),
        in_specs=[_rows(tm, d)] + [_resident(a) for a in consts],
        out_specs=tuple(out_specs),
        compiler_params=pltpu.CompilerParams(dimension_semantics=("parallel",)),
        name="ffn_gla_proj" if gla else "ffn_sb_proj",
    )(x, *consts)


def _ffn_out(x, y, wo, ffn_w, fg, *, final, tm=FFN_ROWS):
    m, d = x.shape
    consts = (wo, *ffn_w, fg)
    return pl.pallas_call(
        functools.partial(_ffn_out_kernel, final=final),
        out_shape=jax.ShapeDtypeStruct((m, d), F32),
        grid=(m // tm,),
        in_specs=[_rows(tm, d), _rows(tm, y.shape[1])] + [_resident(a) for a in consts],
        out_specs=_rows(tm, d),
        compiler_params=pltpu.CompilerParams(dimension_semantics=("parallel",)),
        name="out_proj_ffn",
    )(x, y, *consts)


def _gla_consts():
    c = GLA_CHUNK
    r = np.arange(c)
    tmat = r[None, :] <= r[:, None]
    masks = []
    for lvl in range(GLA_LEVELS):
        n = c >> lvl
        half = n // 2
        same = (r[:, None] // n) == (r[None, :] // n)
        masks.append(same & ((r[:, None] % n) >= half) & ((r[None, :] % n) < half))
    masks.append(r[:, None] == r[None, :])
    return tmat.astype(np.float32), np.stack(masks, 0).astype(np.float32)


def _level_refs(b):
    c, n = b.shape
    sub = 8
    refs = []
    for lvl in range(GLA_LEVELS):
        blk = c >> lvl
        half = blk // 2
        if blk >= 2 * sub:
            refs.append(jnp.concatenate(
                [jnp.broadcast_to(b[m + half:m + half + 1], (blk, n)) for m in range(0, c, blk)], 0))
        else:
            b3 = b.reshape(c // sub, sub, n)
            row = lax.broadcasted_iota(jnp.int32, b3.shape, 1)
            ref = jnp.broadcast_to(b3[:, half:half + 1], b3.shape)
            for m in range(blk, sub, blk):
                ref = jnp.where(row >= m, jnp.broadcast_to(b3[:, m + half:m + half + 1], b3.shape), ref)
            refs.append(ref.reshape(c, n))
    return refs


def _gla_rows(q_ref, k_ref, v_ref, g_ref, la_ref, tm_ref, mask_ref, on_ref, st_ref, o_ref):
    c = GLA_CHUNK
    hk = q_ref.shape[1] // GLA_HEADS
    hv = v_ref.shape[1] // GLA_HEADS
    qscale = float(hk) ** -0.5
    tmat = tm_ref[...]
    onorm = on_ref[...]

    for c0 in range(0, q_ref.shape[0], c):
        sl = slice(c0, c0 + c)
        la = la_ref[sl, :]
        h1 = la.astype(BF16)
        r1 = la - h1.astype(F32)
        h2 = r1.astype(BF16)
        h3 = (r1 - h2.astype(F32)).astype(BF16)
        ball = _dot(tmat, h1) + _dot(tmat, h2) + _dot(tmat, h3)
        refs = _level_refs(ball)

        heads = range(GLA_HEADS)
        ksl = [slice(h * hk, (h + 1) * hk) for h in heads]
        vsl = [slice(h * hv, (h + 1) * hv) for h in heads]
        b = [ball[:, ksl[h]] for h in heads]
        q = [q_ref[sl, ksl[h]].astype(F32) * qscale for h in heads]
        k = [k_ref[sl, ksl[h]].astype(F32) for h in heads]
        v = [v_ref[sl, vsl[h]] for h in heads]
        st = [st_ref[h] for h in heads]

        scores = []
        for h in heads:
            s = mask_ref[GLA_LEVELS] * lax.dot_general(
                q[h].astype(BF16), k[h].astype(BF16), _NT, preferred_element_type=F32)
            for lvl in range(GLA_LEVELS):
                ref = refs[lvl][:, ksl[h]]
                f = jnp.exp(-jnp.abs(b[h] - ref))
                s = s + mask_ref[lvl] * lax.dot_general(
                    (q[h] * f).astype(BF16), (k[h] * f).astype(BF16), _NT,
                    preferred_element_type=F32)
            scores.append(s.astype(BF16))

        outs = []
        for h in heads:
            o = lax.dot_general((q[h] * jnp.exp(b[h])).astype(BF16), st[h].astype(BF16), _NT,
                                preferred_element_type=F32)
            outs.append(o + _dot(scores[h], v[h]))

        for h in heads:
            blast = b[h][c - 1:c, :]
            kd = (k[h] * jnp.exp(blast - b[h])).astype(BF16)
            st_ref[h] = jnp.exp(blast) * st[h] + lax.dot_general(
                v[h], kd, _TN, preferred_element_type=F32)

        for h in heads:
            g = g_ref[sl, vsl[h]].astype(F32)
            y = _rms(outs[h], onorm) * (g * jax.nn.sigmoid(g))
            o_ref[sl, vsl[h]] = y.astype(o_ref.dtype)


def _gla_kernel(q_ref, k_ref, v_ref, g_ref, la_ref, tm_ref, mask_ref, on_ref, o_ref, st_ref):
    @pl.when(pl.program_id(1) == 0)
    def _():
        st_ref[...] = jnp.zeros_like(st_ref)

    _gla_rows(q_ref, k_ref, v_ref, g_ref, la_ref, tm_ref, mask_ref, on_ref, st_ref, o_ref)


def _gla_mix(proj, la, onorm, batch, seq, *, ts=GLA_ROWS):
    m = proj.shape[0]
    dk = la.shape[1]
    dv = (proj.shape[1] - 2 * dk) // 2
    hk = dk // GLA_HEADS
    hv = dv // GLA_HEADS
    nsb = seq // ts
    tmat, masks = _gla_consts()
    rows = lambda width, col: pl.BlockSpec((ts, width), lambda b, s: (b * nsb + s, col))
    consts = (jnp.asarray(tmat, BF16), jnp.asarray(masks), onorm)
    return pl.pallas_call(
        _gla_kernel,
        out_shape=jax.ShapeDtypeStruct((m, dv), BF16),
        grid=(batch, nsb),
        in_specs=[
            rows(dk, 0), rows(dk, 1), rows(dv, (2 * dk) // dv), rows(dv, (2 * dk) // dv + 1),
            rows(dk, 0),
        ] + [pl.BlockSpec(a.shape, lambda b, s, nd=a.ndim: (0,) * nd) for a in consts],
        out_specs=rows(dv, 0),
        scratch_shapes=[pltpu.VMEM((GLA_HEADS, hv, hk), F32)],
        compiler_params=pltpu.CompilerParams(dimension_semantics=("parallel", "arbitrary")),
        name="gla_mix",
    )(proj, proj, proj, proj, la, *consts)


def _sb_kernel(q_ref, k_ref, v_ref, o_ref, acc_ref, *, tq, hd, zscale):
    i = pl.program_id(2)
    nh = acc_ref.shape[0]
    lane = lax.broadcasted_iota(jnp.int32, (tq, LANES), 1)
    cols = [slice((h // 2) * LANES, (h // 2 + 1) * LANES) for h in range(nh)]
    qh = []
    for h in range(nh):
        q = q_ref[:, cols[h]]
        qh.append(jnp.where(lane < hd if h % 2 == 0 else lane >= hd, q, jnp.zeros_like(q)))
    rr = lax.broadcasted_iota(jnp.int32, (tq, tq), 0)
    cc = lax.broadcasted_iota(jnp.int32, (tq, tq), 1)
    suffix = (rr >= cc).astype(BF16)
    causal = (lax.broadcasted_iota(jnp.int32, (tq, tq), 1)
              < lax.broadcasted_iota(jnp.int32, (tq, tq), 0))

    def softplus_stage(h, start, mask):
        z = lax.dot_general(qh[h], k_ref[pl.ds(start, tq), cols[h]], _NT,
                            preferred_element_type=F32) * zscale
        sp = jnp.maximum(z, 0.0) + jnp.log(1.0 + jnp.exp2(-jnp.abs(z))) * LOG2E
        if mask is not None:
            sp = jnp.where(mask, sp, 0.0)
        return z, sp.astype(BF16), jnp.sum(sp, axis=-1, keepdims=True)

    def weight_stage(z, rem, carry, mask):
        w = jnp.exp2(z - rem - carry)
        if mask is not None:
            w = jnp.where(mask, w, 0.0)
        return w.astype(BF16)

    def tiles(jobs, carries):
        first = [softplus_stage(h, start, mask) for h, start, mask in jobs]
        rems = [_dot(pieces, suffix) for _, pieces, _ in first]
        mass = [m for _, _, m in first]
        ws = [weight_stage(first[n][0], rems[n], carries(n, mass), jobs[n][2])
              for n in range(len(jobs))]
        return [_dot(ws[n], v_ref[pl.ds(jobs[n][1], tq), cols[jobs[n][0]]])
                for n in range(len(jobs))], mass

    diag = pl.multiple_of(i * tq, tq)
    prev = pl.multiple_of(jnp.maximum(i - 1, 0) * tq, tq)
    has_prev = (i > 0).astype(F32)
    jobs = [job for h in range(nh) for job in ((h, diag, causal), (h, prev, None))]
    pv, mass = tiles(jobs, lambda n, mass: mass[n - 1] if n % 2 else jnp.zeros((tq, 1), F32))
    carry = []
    for h in range(nh):
        acc_ref[h] = pv[2 * h] + has_prev * pv[2 * h + 1]
        carry.append(mass[2 * h] + mass[2 * h + 1])

    def lowest(cs):
        return functools.reduce(jnp.minimum, [jnp.min(c) for c in cs])

    def live(c):
        t, _, low = c
        return jnp.logical_and(t < i - 1, low < SB_DEAD_MASS_LOG2)

    def body(c):
        t, cs, _ = c
        start = pl.multiple_of((i - 2 - t) * tq, tq)
        pv, mass = tiles([(h, start, None) for h in range(nh)], lambda n, mass: cs[n])
        for h in range(nh):
            acc_ref[h] += pv[h]
        cs = tuple(cs[h] + mass[h] for h in range(nh))
        return t + 1, cs, lowest(cs)

    lax.while_loop(live, body, (jnp.int32(0), tuple(carry), lowest(carry)))
    for p in range(nh // 2):
        o_ref[:, cols[2 * p]] = jnp.where(
            lane < hd, acc_ref[2 * p], acc_ref[2 * p + 1]).astype(o_ref.dtype)


def _sb_mix(qkv, batch, seq, *, tq=256, pairs=SB_PAIRS_PER_STEP):
    m = qkv.shape[0]
    d = qkv.shape[1] // 3
    hd = d // SB_HEADS
    ngrp = d // (pairs * LANES)
    nq = seq // tq
    width = pairs * LANES
    kv_spec = lambda col0: pl.BlockSpec((seq, width), lambda b, p, i: (b, col0 + p),
                                        pipeline_mode=pl.Buffered(1))
    return pl.pallas_call(
        functools.partial(_sb_kernel, tq=tq, hd=hd, zscale=float(hd) ** -0.5 * LOG2E),
        out_shape=jax.ShapeDtypeStruct((m, d), BF16),
        grid=(batch, ngrp, nq),
        in_specs=[
            pl.BlockSpec((tq, width), lambda b, p, i: (b * nq + i, p)),
            kv_spec(ngrp),
            kv_spec(2 * ngrp),
        ],
        out_specs=pl.BlockSpec((tq, width), lambda b, p, i: (b * nq + i, p)),
        scratch_shapes=[pltpu.VMEM((2 * pairs, tq, LANES), F32)],
        compiler_params=pltpu.CompilerParams(
            dimension_semantics=("parallel", "parallel", "arbitrary")),
        name="sb_mix",
    )(qkv, qkv, qkv)


def kernel(x, ffn1_norm, ffn1_w_gate, ffn1_w_up, ffn1_w_down, mix_norm, ffn2_norm, ffn2_w_gate, ffn2_w_up, ffn2_w_down, gla_w_in, gla_w_gk2, gla_b_gk, gla_o_norm, gla_w_out, sb_w_in, sb_w_out, final_norm):
    batch, seq, d = x.shape
    depth = ffn1_norm.shape[0]
    dk = gla_w_gk2.shape[2]
    n_main = gla_w_in.shape[2] - GLA_GATE_RANK
    xs = x.reshape(batch * seq, d)
    fg = final_norm.reshape(1, d)
    for i in range(depth):
        ffn1 = (ffn1_norm[i].reshape(1, d), ffn1_w_gate[i].astype(BF16),
                ffn1_w_up[i].astype(BF16), ffn1_w_down[i].astype(BF16))
        ffn2 = (ffn2_norm[i].reshape(1, d), ffn2_w_gate[i].astype(BF16),
                ffn2_w_up[i].astype(BF16), ffn2_w_down[i].astype(BF16))
        j = i // 2
        mg = mix_norm[i].reshape(1, d)
        if i % 2 == 0:
            w_in = gla_w_in[j]
            wlr = jnp.pad(w_in[:, n_main:], ((0, 0), (0, LANES - GLA_GATE_RANK))).astype(BF16)
            wgk = jnp.pad(gla_w_gk2[j], ((0, LANES - GLA_GATE_RANK), (0, 0))).astype(BF16)
            xs, proj, la = _ffn_in(xs, ffn1, mg, (w_in[:, :n_main].astype(BF16), wlr, wgk,
                                                  gla_b_gk[j].reshape(1, dk)))
            y = _gla_mix(proj, la, gla_o_norm[j].reshape(1, -1), batch, seq)
            w_out = gla_w_out[j]
        else:
            xs, qkv = _ffn_in(xs, ffn1, mg, (sb_w_in[j].astype(BF16),))
            y = _sb_mix(qkv, batch, seq)
            w_out = sb_w_out[j]
        xs = _ffn_out(xs, y, w_out.astype(BF16), ffn2, fg, final=(i == depth - 1))
    return xs.reshape(batch, seq, d)
```

```python
import functools

import numpy as np
import jax
import jax.numpy as jnp
from jax import lax
from jax.experimental import pallas as pl
from jax.experimental.pallas import tpu as pltpu

F32 = jnp.float32
BF16 = jnp.bfloat16

EPS = 1e-6
GLA_HEADS = 4
GLA_GATE_RANK = 16
GLA_GATE_TAU = 16.0
GLA_CHUNK = 64
GLA_LEVELS = 6
GLA_ROWS = 256
SB_HEADS = 16
SB_PAIRS_PER_STEP = 4
LOG2E = 1.4426950408889634
SB_DEAD_MASS_LOG2 = 152.0
LANES = 128
BF16_SUBLANES = 16
MXU_DIM = 256
FFN_ROWS = 512
FFN_CHUNK = MXU_DIM
PROJ_CHUNK = 4 * MXU_DIM

_NT = (((1,), (1,)), ((), ()))
_TN = (((0,), (0,)), ((), ()))


def _rms(x, g):
    ms = jnp.mean(x * x, axis=-1, keepdims=True)
    return x * lax.rsqrt(ms + EPS) * g


def _dot(a, b):
    return jnp.dot(a, b, preferred_element_type=F32)


def _swiglu_half(x, g_ref, wg_ref, wu_ref, wd_ref):
    xn = _rms(x, g_ref[...]).astype(BF16)
    acc = None
    for f0 in range(0, wg_ref.shape[1], FFN_CHUNK):
        a = _dot(xn, wg_ref[:, f0:f0 + FFN_CHUNK])
        u = _dot(xn, wu_ref[:, f0:f0 + FFN_CHUNK])
        h = (a * jax.nn.sigmoid(a)) * u
        part = _dot(h.astype(BF16), wd_ref[f0:f0 + FFN_CHUNK, :])
        acc = part if acc is None else acc + part
    return x + 0.5 * acc


def _project_columns(xn, w_ref, o_ref):
    for c0 in range(0, w_ref.shape[1], PROJ_CHUNK):
        o_ref[:, c0:c0 + PROJ_CHUNK] = _dot(xn, w_ref[:, c0:c0 + PROJ_CHUNK]).astype(o_ref.dtype)


def _ffn_sb_kernel(x_ref, g_ref, wg_ref, wu_ref, wd_ref, mg_ref, w_ref, o_ref, p_ref):
    y = _swiglu_half(x_ref[...], g_ref, wg_ref, wu_ref, wd_ref)
    o_ref[...] = y
    _project_columns(_rms(y, mg_ref[...]).astype(BF16), w_ref, p_ref)


def _ffn_gla_kernel(x_ref, g_ref, wg_ref, wu_ref, wd_ref, mg_ref, w_ref, wlr_ref, wgk_ref,
                    bgk_ref, o_ref, p_ref, la_ref):
    y = _swiglu_half(x_ref[...], g_ref, wg_ref, wu_ref, wd_ref)
    o_ref[...] = y
    hn = _rms(y, mg_ref[...]).astype(BF16)
    lr = _dot(hn, wlr_ref[...])
    z = _dot(lr.astype(BF16), wgk_ref[...]) + bgk_ref[...]
    logsig = jnp.minimum(z, 0.0) - jnp.log1p(jnp.exp(-jnp.abs(z)))
    la_ref[...] = logsig / GLA_GATE_TAU
    _project_columns(hn, w_ref, p_ref)


def _ffn_out_kernel(x_ref, y_ref, wo_ref, g_ref, wg_ref, wu_ref, wd_ref, fg_ref, o_ref, *, final):
    x = x_ref[...] + _dot(y_ref[...], wo_ref[...])
    y = _swiglu_half(x, g_ref, wg_ref, wu_ref, wd_ref)
    if final:
        y = _rms(y, fg_ref[...])
    o_ref[...] = y


def _resident(a):
    return pl.BlockSpec(a.shape, lambda i: (0,) * a.ndim, pipeline_mode=pl.Buffered(1))


def _rows(tm, n):
    return pl.BlockSpec((tm, n), lambda i: (i, 0))


def _cast_rider_specs(riders, nsteps):
    in_specs, out_specs, out_shapes = [], [], []
    for src, layer in riders:
        _, r, c = src.shape
        nblk = max(n for n in (32, 16, 8, 4, 2, 1)
                   if nsteps % n == 0 and r % (n * BF16_SUBLANES) == 0)
        per = nsteps // nblk
        in_specs.append(pl.BlockSpec((None, r // nblk, c),
                                     lambda i, layer=layer, per=per: (layer, i // per, 0)))
        out_specs.append(pl.BlockSpec((r // nblk, c), lambda i, per=per: (i // per, 0)))
        out_shapes.append(jax.ShapeDtypeStruct((r, c), BF16))
    return in_specs, out_specs, out_shapes


def _with_cast_riders(body, n_in, n_rider):
    def kern(*refs):
        n_out = len(refs) - n_in - 2 * n_rider
        ins, rin = refs[:n_in], refs[n_in:n_in + n_rider]
        outs, rout = refs[n_in + n_rider:n_in + n_rider + n_out], refs[n_in + n_rider + n_out:]
        for src, dst in zip(rin, rout):
            dst[...] = src[...].astype(dst.dtype)
        body(*ins, *outs)
    return kern


def _ffn_call(body, name, x_like, consts, out_shapes, out_specs, riders, tm):
    m = x_like[0].shape[0]
    rin, rout, rshape = _cast_rider_specs(riders, m // tm)
    n_in = len(x_like) + len(consts)
    res = pl.pallas_call(
        _with_cast_riders(body, n_in, len(riders)),
        out_shape=tuple(out_shapes) + tuple(rshape),
        grid=(m // tm,),
        in_specs=[_rows(tm, a.shape[1]) for a in x_like] + [_resident(a) for a in consts] + rin,
        out_specs=tuple(out_specs) + tuple(rout),
        compiler_params=pltpu.CompilerParams(dimension_semantics=("parallel",)),
        name=name,
    )(*x_like, *consts, *[src for src, _ in riders])
    return res[:len(out_shapes)], res[len(out_shapes):]


def _ffn_in(x, ffn_w, mg, mix_w, riders, *, tm=FFN_ROWS):
    m, d = x.shape
    n = mix_w[0].shape[1]
    gla = len(mix_w) > 1
    out_shape = [jax.ShapeDtypeStruct((m, d), F32), jax.ShapeDtypeStruct((m, n), BF16)]
    out_specs = [_rows(tm, d), _rows(tm, n)]
    if gla:
        dk = mix_w[2].shape[1]
        out_shape.append(jax.ShapeDtypeStruct((m, dk), F32))
        out_specs.append(_rows(tm, dk))
    return _ffn_call(_ffn_gla_kernel if gla else _ffn_sb_kernel,
                     "ffn_gla_proj" if gla else "ffn_sb_proj",
                     (x,), (*ffn_w, mg, *mix_w), out_shape, out_specs, riders, tm)


def _ffn_out(x, y, wo, ffn_w, fg, riders, *, final, tm=FFN_ROWS):
    m, d = x.shape
    (out,), cast = _ffn_call(functools.partial(_ffn_out_kernel, final=final), "out_proj_ffn",
                             (x, y), (wo, *ffn_w, fg), [jax.ShapeDtypeStruct((m, d), F32)],
                             [_rows(tm, d)], riders, tm)
    return out, cast


def _gla_consts():
    c = GLA_CHUNK
    r = np.arange(c)
    tmat = r[None, :] <= r[:, None]
    masks = []
    for lvl in range(GLA_LEVELS):
        n = c >> lvl
        half = n // 2
        same = (r[:, None] // n) == (r[None, :] // n)
        masks.append(same & ((r[:, None] % n) >= half) & ((r[None, :] % n) < half))
    masks.append(r[:, None] == r[None, :])
    return tmat.astype(np.float32), np.stack(masks, 0).astype(np.float32)


def _level_refs(b):
    c, n = b.shape
    sub = 8
    refs = []
    for lvl in range(GLA_LEVELS):
        blk = c >> lvl
        half = blk // 2
        if blk >= 2 * sub:
            refs.append(jnp.concatenate(
                [jnp.broadcast_to(b[m + half:m + half + 1], (blk, n)) for m in range(0, c, blk)], 0))
        else:
            b3 = b.reshape(c // sub, sub, n)
            row = lax.broadcasted_iota(jnp.int32, b3.shape, 1)
            ref = jnp.broadcast_to(b3[:, half:half + 1], b3.shape)
            for m in range(blk, sub, blk):
                ref = jnp.where(row >= m, jnp.broadcast_to(b3[:, m + half:m + half + 1], b3.shape), ref)
            refs.append(ref.reshape(c, n))
    return refs


def _gla_rows(q_ref, k_ref, v_ref, g_ref, la_ref, tm_ref, mask_ref, on_ref, st_ref, o_ref):
    c = GLA_CHUNK
    hk = q_ref.shape[1] // GLA_HEADS
    hv = v_ref.shape[1] // GLA_HEADS
    qscale = float(hk) ** -0.5
    tmat = tm_ref[...]
    onorm = on_ref[...]

    for c0 in range(0, q_ref.shape[0], c):
        sl = slice(c0, c0 + c)
        la = la_ref[sl, :]
        h1 = la.astype(BF16)
        r1 = la - h1.astype(F32)
        h2 = r1.astype(BF16)
        h3 = (r1 - h2.astype(F32)).astype(BF16)
        ball = _dot(tmat, h1) + _dot(tmat, h2) + _dot(tmat, h3)
        refs = _level_refs(ball)

        heads = range(GLA_HEADS)
        ksl = [slice(h * hk, (h + 1) * hk) for h in heads]
        vsl = [slice(h * hv, (h + 1) * hv) for h in heads]
        b = [ball[:, ksl[h]] for h in heads]
        q = [q_ref[sl, ksl[h]].astype(F32) * qscale for h in heads]
        k = [k_ref[sl, ksl[h]].astype(F32) for h in heads]
        v = [v_ref[sl, vsl[h]] for h in heads]
        st = [st_ref[h] for h in heads]

        scores = []
        for h in heads:
            s = mask_ref[GLA_LEVELS] * lax.dot_general(
                q[h].astype(BF16), k[h].astype(BF16), _NT, preferred_element_type=F32)
            for lvl in range(GLA_LEVELS):
                ref = refs[lvl][:, ksl[h]]
                f = jnp.exp(-jnp.abs(b[h] - ref))
                s = s + mask_ref[lvl] * lax.dot_general(
                    (q[h] * f).astype(BF16), (k[h] * f).astype(BF16), _NT,
                    preferred_element_type=F32)
            scores.append(s.astype(BF16))

        outs = []
        for h in heads:
            o = lax.dot_general((q[h] * jnp.exp(b[h])).astype(BF16), st[h].astype(BF16), _NT,
                                preferred_element_type=F32)
            outs.append(o + _dot(scores[h], v[h]))

        for h in heads:
            blast = b[h][c - 1:c, :]
            kd = (k[h] * jnp.exp(blast - b[h])).astype(BF16)
            st_ref[h] = jnp.exp(blast) * st[h] + lax.dot_general(
                v[h], kd, _TN, preferred_element_type=F32)

        for h in heads:
            g = g_ref[sl, vsl[h]].astype(F32)
            y = _rms(outs[h], onorm) * (g * jax.nn.sigmoid(g))
            o_ref[sl, vsl[h]] = y.astype(o_ref.dtype)


def _gla_kernel(q_ref, k_ref, v_ref, g_ref, la_ref, tm_ref, mask_ref, on_ref, o_ref, st_ref):
    @pl.when(pl.program_id(1) == 0)
    def _():
        st_ref[...] = jnp.zeros_like(st_ref)

    _gla_rows(q_ref, k_ref, v_ref, g_ref, la_ref, tm_ref, mask_ref, on_ref, st_ref, o_ref)


def _gla_mix(proj, la, onorm, batch, seq, *, ts=GLA_ROWS):
    m = proj.shape[0]
    dk = la.shape[1]
    dv = (proj.shape[1] - 2 * dk) // 2
    hk = dk // GLA_HEADS
    hv = dv // GLA_HEADS
    nsb = seq // ts
    tmat, masks = _gla_consts()
    rows = lambda width, col: pl.BlockSpec((ts, width), lambda b, s: (b * nsb + s, col))
    consts = (jnp.asarray(tmat, BF16), jnp.asarray(masks), onorm)
    return pl.pallas_call(
        _gla_kernel,
        out_shape=jax.ShapeDtypeStruct((m, dv), BF16),
        grid=(batch, nsb),
        in_specs=[
            rows(dk, 0), rows(dk, 1), rows(dv, (2 * dk) // dv), rows(dv, (2 * dk) // dv + 1),
            rows(dk, 0),
        ] + [pl.BlockSpec(a.shape, lambda b, s, nd=a.ndim: (0,) * nd) for a in consts],
        out_specs=rows(dv, 0),
        scratch_shapes=[pltpu.VMEM((GLA_HEADS, hv, hk), F32)],
        compiler_params=pltpu.CompilerParams(dimension_semantics=("parallel", "arbitrary")),
        name="gla_mix",
    )(proj, proj, proj, proj, la, *consts)


def _sb_kernel(q_ref, k_ref, v_ref, o_ref, acc_ref, *, tq, hd, zscale):
    i = pl.program_id(2)
    nh = acc_ref.shape[0]
    lane = lax.broadcasted_iota(jnp.int32, (tq, LANES), 1)
    cols = [slice((h // 2) * LANES, (h // 2 + 1) * LANES) for h in range(nh)]
    qh = []
    for h in range(nh):
        q = q_ref[:, cols[h]]
        qh.append(jnp.where(lane < hd if h % 2 == 0 else lane >= hd, q, jnp.zeros_like(q)))
    rr = lax.broadcasted_iota(jnp.int32, (tq, tq), 0)
    cc = lax.broadcasted_iota(jnp.int32, (tq, tq), 1)
    suffix = (rr >= cc).astype(BF16)
    causal = (lax.broadcasted_iota(jnp.int32, (tq, tq), 1)
              < lax.broadcasted_iota(jnp.int32, (tq, tq), 0))

    def softplus_stage(h, start, mask):
        z = lax.dot_general(qh[h], k_ref[pl.ds(start, tq), cols[h]], _NT,
                            preferred_element_type=F32) * zscale
        sp = jnp.maximum(z, 0.0) + jnp.log(1.0 + jnp.exp2(-jnp.abs(z))) * LOG2E
        if mask is not None:
            sp = jnp.where(mask, sp, 0.0)
        return z, sp.astype(BF16), jnp.sum(sp, axis=-1, keepdims=True)

    def weight_stage(z, rem, carry, mask):
        w = jnp.exp2(z - rem - carry)
        if mask is not None:
            w = jnp.where(mask, w, 0.0)
        return w.astype(BF16)

    def tiles(jobs, carries):
        first = [softplus_stage(h, start, mask) for h, start, mask in jobs]
        rems = [_dot(pieces, suffix) for _, pieces, _ in first]
        mass = [m for _, _, m in first]
        ws = [weight_stage(first[n][0], rems[n], carries(n, mass), jobs[n][2])
              for n in range(len(jobs))]
        return [_dot(ws[n], v_ref[pl.ds(jobs[n][1], tq), cols[jobs[n][0]]])
                for n in range(len(jobs))], mass

    diag = pl.multiple_of(i * tq, tq)
    prev = pl.multiple_of(jnp.maximum(i - 1, 0) * tq, tq)
    has_prev = (i > 0).astype(F32)
    jobs = [job for h in range(nh) for job in ((h, diag, causal), (h, prev, None))]
    pv, mass = tiles(jobs, lambda n, mass: mass[n - 1] if n % 2 else jnp.zeros((tq, 1), F32))
    carry = []
    for h in range(nh):
        acc_ref[h] = pv[2 * h] + has_prev * pv[2 * h + 1]
        carry.append(mass[2 * h] + mass[2 * h + 1])

    def lowest(cs):
        return functools.reduce(jnp.minimum, [jnp.min(c) for c in cs])

    def live(c):
        t, _, low = c
        return jnp.logical_and(t < i - 1, low < SB_DEAD_MASS_LOG2)

    def body(c):
        t, cs, _ = c
        start = pl.multiple_of((i - 2 - t) * tq, tq)
        pv, mass = tiles([(h, start, None) for h in range(nh)], lambda n, mass: cs[n])
        for h in range(nh):
            acc_ref[h] += pv[h]
        cs = tuple(cs[h] + mass[h] for h in range(nh))
        return t + 1, cs, lowest(cs)

    lax.while_loop(live, body, (jnp.int32(0), tuple(carry), lowest(carry)))
    for p in range(nh // 2):
        o_ref[:, cols[2 * p]] = jnp.where(
            lane < hd, acc_ref[2 * p], acc_ref[2 * p + 1]).astype(o_ref.dtype)


def _sb_mix(qkv, batch, seq, *, tq=256, pairs=SB_PAIRS_PER_STEP):
    m = qkv.shape[0]
    d = qkv.shape[1] // 3
    hd = d // SB_HEADS
    ngrp = d // (pairs * LANES)
    nq = seq // tq
    width = pairs * LANES
    kv_spec = lambda col0: pl.BlockSpec((seq, width), lambda b, p, i: (b, col0 + p),
                                        pipeline_mode=pl.Buffered(1))
    return pl.pallas_call(
        functools.partial(_sb_kernel, tq=tq, hd=hd, zscale=float(hd) ** -0.5 * LOG2E),
        out_shape=jax.ShapeDtypeStruct((m, d), BF16),
        grid=(batch, ngrp, nq),
        in_specs=[
            pl.BlockSpec((tq, width), lambda b, p, i: (b * nq + i, p)),
            kv_spec(ngrp),
            kv_spec(2 * ngrp),
        ],
        out_specs=pl.BlockSpec((tq, width), lambda b, p, i: (b * nq + i, p)),
        scratch_shapes=[pltpu.VMEM((2 * pairs, tq, LANES), F32)],
        compiler_params=pltpu.CompilerParams(
            dimension_semantics=("parallel", "parallel", "arbitrary")),
        name="sb_mix",
    )(qkv, qkv, qkv)


def kernel(x, ffn1_norm, ffn1_w_gate, ffn1_w_up, ffn1_w_down, mix_norm, ffn2_norm, ffn2_w_gate, ffn2_w_up, ffn2_w_down, gla_w_in, gla_w_gk2, gla_b_gk, gla_o_norm, gla_w_out, sb_w_in, sb_w_out, final_norm):
    batch, seq, d = x.shape
    depth = ffn1_norm.shape[0]
    dk = gla_w_gk2.shape[2]
    n_main = gla_w_in.shape[2] - GLA_GATE_RANK
    xs = x.reshape(batch * seq, d)
    fg = final_norm.reshape(1, d)
    ffn1_w = tuple(w[0].astype(BF16) for w in (ffn1_w_gate, ffn1_w_up, ffn1_w_down))
    mix_in = gla_w_in[0].astype(BF16)
    for i in range(depth):
        j = i // 2
        gla = i % 2 == 0
        mg = mix_norm[i].reshape(1, d)
        ffn1 = (ffn1_norm[i].reshape(1, d), *ffn1_w)
        riders = [(ffn2_w_gate, i), (ffn2_w_up, i), (ffn2_w_down, i),
                  (gla_w_out if gla else sb_w_out, j)]
        if gla:
            wlr = jnp.pad(mix_in[:, n_main:], ((0, 0), (0, LANES - GLA_GATE_RANK)))
            wgk = jnp.pad(gla_w_gk2[j], ((0, LANES - GLA_GATE_RANK), (0, 0))).astype(BF16)
            (xs, proj, la), cast = _ffn_in(
                xs, ffn1, mg, (mix_in[:, :n_main], wlr, wgk, gla_b_gk[j].reshape(1, dk)), riders)
            y = _gla_mix(proj, la, gla_o_norm[j].reshape(1, -1), batch, seq)
        else:
            (xs, qkv), cast = _ffn_in(xs, ffn1, mg, (mix_in,), riders)
            y = _sb_mix(qkv, batch, seq)
        ffn2 = (ffn2_norm[i].reshape(1, d), *cast[:3])
        riders = []
        if i + 1 < depth:
            riders = [(ffn1_w_gate, i + 1), (ffn1_w_up, i + 1), (ffn1_w_down, i + 1),
                      (sb_w_in if gla else gla_w_in, (i + 1) // 2)]
        xs, cast_next = _ffn_out(xs, y, cast[3], ffn2, fg, riders, final=(i == depth - 1))
        if riders:
            ffn1_w, mix_in = cast_next[:3], cast_next[3]
    return xs.reshape(batch, seq, d)
```

```python
import functools

import numpy as np
import jax
import jax.numpy as jnp
from jax import lax
from jax.experimental import pallas as pl
from jax.experimental.pallas import tpu as pltpu

F32 = jnp.float32
BF16 = jnp.bfloat16

EPS = 1e-6
GLA_HEADS = 4
GLA_GATE_RANK = 16
GLA_GATE_TAU = 16.0
GLA_CHUNK = 64
GLA_LEVELS = 6
GLA_ROWS = 256
SB_HEADS = 16
SB_EXP2_CAP = 64.0
SB_PAIRS_PER_STEP = 4
LOG2E = 1.4426950408889634
SB_DEAD_MASS_LOG2 = 152.0
LANES = 128
BF16_SUBLANES = 16
MXU_DIM = 256
FFN_ROWS = 512
FFN_CHUNK = MXU_DIM
PROJ_CHUNK = 4 * MXU_DIM

_NT = (((1,), (1,)), ((), ()))
_TN = (((0,), (0,)), ((), ()))


def _rms(x, g):
    ms = jnp.mean(x * x, axis=-1, keepdims=True)
    return x * lax.rsqrt(ms + EPS) * g


def _dot(a, b):
    return jnp.dot(a, b, preferred_element_type=F32)


def _swiglu_half(x, g_ref, wg_ref, wu_ref, wd_ref):
    xn = _rms(x, g_ref[...]).astype(BF16)
    acc = None
    for f0 in range(0, wg_ref.shape[1], FFN_CHUNK):
        a = _dot(xn, wg_ref[:, f0:f0 + FFN_CHUNK])
        u = _dot(xn, wu_ref[:, f0:f0 + FFN_CHUNK])
        h = (a * jax.nn.sigmoid(a)) * u
        part = _dot(h.astype(BF16), wd_ref[f0:f0 + FFN_CHUNK, :])
        acc = part if acc is None else acc + part
    return x + 0.5 * acc


def _project_columns(xn, w_ref, o_ref):
    for c0 in range(0, w_ref.shape[1], PROJ_CHUNK):
        o_ref[:, c0:c0 + PROJ_CHUNK] = _dot(xn, w_ref[:, c0:c0 + PROJ_CHUNK]).astype(o_ref.dtype)


def _ffn_sb_kernel(x_ref, g_ref, wg_ref, wu_ref, wd_ref, mg_ref, w_ref, o_ref, p_ref):
    y = _swiglu_half(x_ref[...], g_ref, wg_ref, wu_ref, wd_ref)
    o_ref[...] = y
    _project_columns(_rms(y, mg_ref[...]).astype(BF16), w_ref, p_ref)


def _ffn_gla_kernel(x_ref, g_ref, wg_ref, wu_ref, wd_ref, mg_ref, w_ref, wlr_ref, wgk_ref,
                    bgk_ref, o_ref, p_ref, la_ref):
    y = _swiglu_half(x_ref[...], g_ref, wg_ref, wu_ref, wd_ref)
    o_ref[...] = y
    hn = _rms(y, mg_ref[...]).astype(BF16)
    lr = _dot(hn, wlr_ref[...])
    z = _dot(lr.astype(BF16), wgk_ref[...]) + bgk_ref[...]
    logsig = jnp.minimum(z, 0.0) - jnp.log1p(jnp.exp(-jnp.abs(z)))
    la_ref[...] = logsig / GLA_GATE_TAU
    _project_columns(hn, w_ref, p_ref)


def _ffn_out_kernel(x_ref, y_ref, wo_ref, g_ref, wg_ref, wu_ref, wd_ref, fg_ref, o_ref, *, final):
    x = x_ref[...] + _dot(y_ref[...], wo_ref[...])
    y = _swiglu_half(x, g_ref, wg_ref, wu_ref, wd_ref)
    if final:
        y = _rms(y, fg_ref[...])
    o_ref[...] = y


def _resident(a):
    return pl.BlockSpec(a.shape, lambda i: (0,) * a.ndim, pipeline_mode=pl.Buffered(1))


def _rows(tm, n):
    return pl.BlockSpec((tm, n), lambda i: (i, 0))


def _cast_rider_specs(riders, nsteps):
    in_specs, out_specs, out_shapes = [], [], []
    for src, layer in riders:
        _, r, c = src.shape
        nblk = max(n for n in (32, 16, 8, 4, 2, 1)
                   if nsteps % n == 0 and r % (n * BF16_SUBLANES) == 0)
        per = nsteps // nblk
        in_specs.append(pl.BlockSpec((None, r // nblk, c),
                                     lambda i, layer=layer, per=per: (layer, i // per, 0)))
        out_specs.append(pl.BlockSpec((r // nblk, c), lambda i, per=per: (i // per, 0)))
        out_shapes.append(jax.ShapeDtypeStruct((r, c), BF16))
    return in_specs, out_specs, out_shapes


def _with_cast_riders(body, n_in, n_rider):
    def kern(*refs):
        n_out = len(refs) - n_in - 2 * n_rider
        ins, rin = refs[:n_in], refs[n_in:n_in + n_rider]
        outs, rout = refs[n_in + n_rider:n_in + n_rider + n_out], refs[n_in + n_rider + n_out:]
        for src, dst in zip(rin, rout):
            dst[...] = src[...].astype(dst.dtype)
        body(*ins, *outs)
    return kern


def _ffn_call(body, name, x_like, consts, out_shapes, out_specs, riders, tm):
    m = x_like[0].shape[0]
    rin, rout, rshape = _cast_rider_specs(riders, m // tm)
    n_in = len(x_like) + len(consts)
    res = pl.pallas_call(
        _with_cast_riders(body, n_in, len(riders)),
        out_shape=tuple(out_shapes) + tuple(rshape),
        grid=(m // tm,),
        in_specs=[_rows(tm, a.shape[1]) for a in x_like] + [_resident(a) for a in consts] + rin,
        out_specs=tuple(out_specs) + tuple(rout),
        compiler_params=pltpu.CompilerParams(dimension_semantics=("parallel",)),
        name=name,
    )(*x_like, *consts, *[src for src, _ in riders])
    return res[:len(out_shapes)], res[len(out_shapes):]


def _ffn_in(x, ffn_w, mg, mix_w, riders, *, tm=FFN_ROWS):
    m, d = x.shape
    n = mix_w[0].shape[1]
    gla = len(mix_w) > 1
    out_shape = [jax.ShapeDtypeStruct((m, d), F32), jax.ShapeDtypeStruct((m, n), BF16)]
    out_specs = [_rows(tm, d), _rows(tm, n)]
    if gla:
        dk = mix_w[2].shape[1]
        out_shape.append(jax.ShapeDtypeStruct((m, dk), F32))
        out_specs.append(_rows(tm, dk))
    return _ffn_call(_ffn_gla_kernel if gla else _ffn_sb_kernel,
                     "ffn_gla_proj" if gla else "ffn_sb_proj",
                     (x,), (*ffn_w, mg, *mix_w), out_shape, out_specs, riders, tm)


def _ffn_out(x, y, wo, ffn_w, fg, riders, *, final, tm=FFN_ROWS):
    m, d = x.shape
    (out,), cast = _ffn_call(functools.partial(_ffn_out_kernel, final=final), "out_proj_ffn",
                             (x, y), (wo, *ffn_w, fg), [jax.ShapeDtypeStruct((m, d), F32)],
                             [_rows(tm, d)], riders, tm)
    return out, cast


def _gla_consts():
    c = GLA_CHUNK
    r = np.arange(c)
    tmat = r[None, :] <= r[:, None]
    masks, signs = [], []
    for lvl in range(GLA_LEVELS):
        n = c >> lvl
        half = n // 2
        same = (r[:, None] // n) == (r[None, :] // n)
        masks.append(same & ((r[:, None] % n) >= half) & ((r[None, :] % n) < half))
        signs.append(np.where((r % n) >= half, LOG2E, -LOG2E))
    masks.append(r[:, None] == r[None, :])
    signs = np.broadcast_to(np.stack(signs, 0)[:, :, None], (GLA_LEVELS, c, LANES))
    return (tmat.astype(np.float32), np.stack(masks, 0).astype(np.float32),
            np.ascontiguousarray(signs, dtype=np.float32))


def _level_refs(b):
    c, n = b.shape
    sub = 8
    refs = []
    for lvl in range(GLA_LEVELS):
        blk = c >> lvl
        half = blk // 2
        if blk >= 2 * sub:
            refs.append(jnp.concatenate(
                [jnp.broadcast_to(b[m + half:m + half + 1], (blk, n)) for m in range(0, c, blk)], 0))
        else:
            b3 = b.reshape(c // sub, sub, n)
            row = lax.broadcasted_iota(jnp.int32, b3.shape, 1)
            ref = jnp.broadcast_to(b3[:, half:half + 1], b3.shape)
            for m in range(blk, sub, blk):
                ref = jnp.where(row >= m, jnp.broadcast_to(b3[:, m + half:m + half + 1], b3.shape), ref)
            refs.append(ref.reshape(c, n))
    return refs


def _gla_rows(q_ref, k_ref, v_ref, g_ref, la_ref, tm_ref, mask_ref, sg_ref, on_ref, st_ref, o_ref):
    c = GLA_CHUNK
    hk = q_ref.shape[1] // GLA_HEADS
    hv = v_ref.shape[1] // GLA_HEADS
    qscale = float(hk) ** -0.5
    tmat = tm_ref[...]
    onorm = on_ref[...]

    for c0 in range(0, q_ref.shape[0], c):
        sl = slice(c0, c0 + c)
        la = la_ref[sl, :]
        h1 = la.astype(BF16)
        r1 = la - h1.astype(F32)
        h2 = r1.astype(BF16)
        h3 = (r1 - h2.astype(F32)).astype(BF16)
        ball = _dot(tmat, h1) + _dot(tmat, h2) + _dot(tmat, h3)
        refs = _level_refs(ball)

        heads = range(GLA_HEADS)
        ksl = [slice(h * hk, (h + 1) * hk) for h in heads]
        vsl = [slice(h * hv, (h + 1) * hv) for h in heads]
        b = [ball[:, ksl[h]] for h in heads]
        q = [q_ref[sl, ksl[h]].astype(F32) * qscale for h in heads]
        k = [k_ref[sl, ksl[h]].astype(F32) for h in heads]
        v = [v_ref[sl, vsl[h]] for h in heads]
        st = [st_ref[h] for h in heads]

        scores = []
        for h in heads:
            s = mask_ref[GLA_LEVELS] * lax.dot_general(
                q[h].astype(BF16), k[h].astype(BF16), _NT, preferred_element_type=F32)
            for lvl in range(GLA_LEVELS):
                ref = refs[lvl][:, ksl[h]]
                f = jnp.exp2((b[h] - ref) * sg_ref[lvl])
                s = s + mask_ref[lvl] * lax.dot_general(
                    (q[h] * f).astype(BF16), (k[h] * f).astype(BF16), _NT,
                    preferred_element_type=F32)
            scores.append(s.astype(BF16))

        outs = []
        for h in heads:
            o = lax.dot_general((q[h] * jnp.exp(b[h])).astype(BF16), st[h].astype(BF16), _NT,
                                preferred_element_type=F32)
            outs.append(o + _dot(scores[h], v[h]))

        for h in heads:
            blast = b[h][c - 1:c, :]
            kd = (k[h] * jnp.exp(blast - b[h])).astype(BF16)
            st_ref[h] = jnp.exp(blast) * st[h] + lax.dot_general(
                v[h], kd, _TN, preferred_element_type=F32)

        for h in heads:
            g = g_ref[sl, vsl[h]].astype(F32)
            y = _rms(outs[h], onorm) * (g * jax.nn.sigmoid(g))
            o_ref[sl, vsl[h]] = y.astype(o_ref.dtype)


def _gla_kernel(q_ref, k_ref, v_ref, g_ref, la_ref, tm_ref, mask_ref, sg_ref, on_ref, o_ref,
                st_ref):
    @pl.when(pl.program_id(1) == 0)
    def _():
        st_ref[...] = jnp.zeros_like(st_ref)

    _gla_rows(q_ref, k_ref, v_ref, g_ref, la_ref, tm_ref, mask_ref, sg_ref, on_ref, st_ref, o_ref)


def _gla_mix(proj, la, onorm, batch, seq, *, ts=GLA_ROWS):
    m = proj.shape[0]
    dk = la.shape[1]
    dv = (proj.shape[1] - 2 * dk) // 2
    hk = dk // GLA_HEADS
    hv = dv // GLA_HEADS
    nsb = seq // ts
    tmat, masks, signs = _gla_consts()
    rows = lambda width, col: pl.BlockSpec((ts, width), lambda b, s: (b * nsb + s, col))
    consts = (jnp.asarray(tmat, BF16), jnp.asarray(masks), jnp.asarray(signs), onorm)
    return pl.pallas_call(
        _gla_kernel,
        out_shape=jax.ShapeDtypeStruct((m, dv), BF16),
        grid=(batch, nsb),
        in_specs=[
            rows(dk, 0), rows(dk, 1), rows(dv, (2 * dk) // dv), rows(dv, (2 * dk) // dv + 1),
            rows(dk, 0),
        ] + [pl.BlockSpec(a.shape, lambda b, s, nd=a.ndim: (0,) * nd) for a in consts],
        out_specs=rows(dv, 0),
        scratch_shapes=[pltpu.VMEM((GLA_HEADS, hv, hk), F32)],
        compiler_params=pltpu.CompilerParams(dimension_semantics=("parallel", "arbitrary")),
        name="gla_mix",
    )(proj, proj, proj, proj, la, *consts)


def _sb_kernel(q_ref, k_ref, v_ref, o_ref, acc_ref, *, tq, hd, zscale):
    i = pl.program_id(2)
    nh = acc_ref.shape[0]
    lane = lax.broadcasted_iota(jnp.int32, (tq, LANES), 1)
    cols = [slice((h // 2) * LANES, (h // 2 + 1) * LANES) for h in range(nh)]
    qh = []
    for h in range(nh):
        q = q_ref[:, cols[h]]
        qh.append(jnp.where(lane < hd if h % 2 == 0 else lane >= hd, q, jnp.zeros_like(q)))
    rr = lax.broadcasted_iota(jnp.int32, (tq, tq), 0)
    cc = lax.broadcasted_iota(jnp.int32, (tq, tq), 1)
    suffix = (rr >= cc).astype(BF16)
    causal = (lax.broadcasted_iota(jnp.int32, (tq, tq), 1)
              < lax.broadcasted_iota(jnp.int32, (tq, tq), 0))

    def softplus_stage(h, start, mask):
        z = lax.dot_general(qh[h], k_ref[pl.ds(start, tq), cols[h]], _NT,
                            preferred_element_type=F32) * zscale
        sp = jnp.maximum(z, jnp.log(1.0 + jnp.exp2(jnp.minimum(z, SB_EXP2_CAP))) * LOG2E)
        if mask is not None:
            sp = jnp.where(mask, sp, 0.0)
        return z, sp.astype(BF16), jnp.sum(sp, axis=-1, keepdims=True)

    def weight_stage(z, rem, carry, mask):
        w = jnp.exp2(z - rem - carry)
        if mask is not None:
            w = jnp.where(mask, w, 0.0)
        return w.astype(BF16)

    def tiles(head_jobs, carries):
        flat = [(h, *job) for h, jobs in enumerate(head_jobs) for job in jobs]
        first = [softplus_stage(h, start, mask) for h, start, mask, _ in flat]
        rems = [_dot(sp16, suffix) for _, sp16, _ in first]
        mass = [m for _, _, m in first]
        ws = [weight_stage(first[n][0], rems[n], carries(n, mass), flat[n][2])
              for n in range(len(flat))]
        pvs, n = [], 0
        for h, jobs in enumerate(head_jobs):
            vs = []
            for start, _, keep in jobs:
                v = v_ref[pl.ds(start, tq), cols[h]]
                vs.append(v if keep is None else jnp.where(keep, v, jnp.zeros_like(v)))
            pvs.append(_dot(jnp.concatenate(ws[n:n + len(jobs)], axis=1),
                            jnp.concatenate(vs, axis=0)))
            n += len(jobs)
        return pvs, mass

    diag = pl.multiple_of(i * tq, tq)
    prev = pl.multiple_of(jnp.maximum(i - 1, 0) * tq, tq)
    window = [(diag, causal, None), (prev, None, i > 0)]
    pv, mass = tiles([window] * nh,
                     lambda n, mass: mass[n - 1] if n % 2 else jnp.zeros((tq, 1), F32))
    carry = []
    for h in range(nh):
        acc_ref[h] = pv[h]
        carry.append(mass[2 * h] + mass[2 * h + 1])

    def lowest(cs):
        return functools.reduce(jnp.minimum, [jnp.min(c) for c in cs])

    def live(c):
        t, _, low = c
        return jnp.logical_and(t < i - 1, low < SB_DEAD_MASS_LOG2)

    def body(c):
        t, cs, _ = c
        start = pl.multiple_of((i - 2 - t) * tq, tq)
        pv, mass = tiles([[(start, None, None)]] * nh, lambda n, mass: cs[n])
        for h in range(nh):
            acc_ref[h] += pv[h]
        cs = tuple(cs[h] + mass[h] for h in range(nh))
        return t + 1, cs, lowest(cs)

    lax.while_loop(live, body, (jnp.int32(0), tuple(carry), lowest(carry)))
    for p in range(nh // 2):
        o_ref[:, cols[2 * p]] = jnp.where(
            lane < hd, acc_ref[2 * p], acc_ref[2 * p + 1]).astype(o_ref.dtype)


def _sb_mix(qkv, batch, seq, *, tq=256, pairs=SB_PAIRS_PER_STEP):
    m = qkv.shape[0]
    d = qkv.shape[1] // 3
    hd = d // SB_HEADS
    ngrp = d // (pairs * LANES)
    nq = seq // tq
    width = pairs * LANES
    kv_spec = lambda col0: pl.BlockSpec((seq, width), lambda b, p, i: (b, col0 + p),
                                        pipeline_mode=pl.Buffered(1))
    return pl.pallas_call(
        functools.partial(_sb_kernel, tq=tq, hd=hd, zscale=float(hd) ** -0.5 * LOG2E),
        out_shape=jax.ShapeDtypeStruct((m, d), BF16),
        grid=(batch, ngrp, nq),
        in_specs=[
            pl.BlockSpec((tq, width), lambda b, p, i: (b * nq + i, p)),
            kv_spec(ngrp),
            kv_spec(2 * ngrp),
        ],
        out_specs=pl.BlockSpec((tq, width), lambda b, p, i: (b * nq + i, p)),
        scratch_shapes=[pltpu.VMEM((2 * pairs, tq, LANES), F32)],
        compiler_params=pltpu.CompilerParams(
            dimension_semantics=("parallel", "parallel", "arbitrary")),
        name="sb_mix",
    )(qkv, qkv, qkv)


def kernel(x, ffn1_norm, ffn1_w_gate, ffn1_w_up, ffn1_w_down, mix_norm, ffn2_norm, ffn2_w_gate, ffn2_w_up, ffn2_w_down, gla_w_in, gla_w_gk2, gla_b_gk, gla_o_norm, gla_w_out, sb_w_in, sb_w_out, final_norm):
    batch, seq, d = x.shape
    depth = ffn1_norm.shape[0]
    dk = gla_w_gk2.shape[2]
    n_main = gla_w_in.shape[2] - GLA_GATE_RANK
    xs = x.reshape(batch * seq, d)
    fg = final_norm.reshape(1, d)
    ffn1_w = tuple(w[0].astype(BF16) for w in (ffn1_w_gate, ffn1_w_up, ffn1_w_down))
    mix_in = gla_w_in[0].astype(BF16)
    for i in range(depth):
        j = i // 2
        gla = i % 2 == 0
        mg = mix_norm[i].reshape(1, d)
        ffn1 = (ffn1_norm[i].reshape(1, d), *ffn1_w)
        riders = [(ffn2_w_gate, i), (ffn2_w_up, i), (ffn2_w_down, i),
                  (gla_w_out if gla else sb_w_out, j)]
        if gla:
            wlr = jnp.pad(mix_in[:, n_main:], ((0, 0), (0, LANES - GLA_GATE_RANK)))
            wgk = jnp.pad(gla_w_gk2[j], ((0, LANES - GLA_GATE_RANK), (0, 0))).astype(BF16)
            (xs, proj, la), cast = _ffn_in(
                xs, ffn1, mg, (mix_in[:, :n_main], wlr, wgk, gla_b_gk[j].reshape(1, dk)), riders)
            y = _gla_mix(proj, la, gla_o_norm[j].reshape(1, -1), batch, seq)
        else:
            (xs, qkv), cast = _ffn_in(xs, ffn1, mg, (mix_in,), riders)
            y = _sb_mix(qkv, batch, seq)
        ffn2 = (ffn2_norm[i].reshape(1, d), *cast[:3])
        riders = []
        if i + 1 < depth:
            riders = [(ffn1_w_gate, i + 1), (ffn1_w_up, i + 1), (ffn1_w_down, i + 1),
                      (sb_w_in if gla else gla_w_in, (i + 1) // 2)]
        xs, cast_next = _ffn_out(xs, y, cast[3], ffn2, fg, riders, final=(i == depth - 1))
        if riders:
            ffn1_w, mix_in = cast_next[:3], cast_next[3]
    return xs.reshape(batch, seq, d)
```

```python
import functools

import numpy as np
import jax
import jax.numpy as jnp
from jax import lax
from jax.experimental import pallas as pl
from jax.experimental.pallas import tpu as pltpu

F32 = jnp.float32
BF16 = jnp.bfloat16

EPS = 1e-6
GLA_HEADS = 4
GLA_GATE_RANK = 16
GLA_GATE_TAU = 16.0
GLA_CHUNK = 64
GLA_LEVELS = 6
GLA_ROWS = 256
SB_HEADS = 16
SB_EXP2_CAP = 64.0
SB_PAIRS_PER_STEP = 4
LOG2E = 1.4426950408889634
SB_HEAD_DIM = 64
SB_ZSCALE = SB_HEAD_DIM ** -0.5 * LOG2E
SB_DEAD_MASS_LOG2 = 152.0
LANES = 128
BF16_SUBLANES = 16
MXU_DIM = 256
FFN_ROWS = 512
FFN_CHUNK = MXU_DIM
PROJ_CHUNK = 4 * MXU_DIM

_NT = (((1,), (1,)), ((), ()))
_TN = (((0,), (0,)), ((), ()))


def _rms(x, g):
    ms = jnp.mean(x * x, axis=-1, keepdims=True)
    return x * lax.rsqrt(ms + EPS) * g


def _dot(a, b):
    return jnp.dot(a, b, preferred_element_type=F32)


def _swiglu_half(x, g_ref, wg_ref, wu_ref, wd_ref):
    xn = _rms(x, g_ref[...]).astype(BF16)
    acc = None
    for f0 in range(0, wg_ref.shape[1], FFN_CHUNK):
        a = _dot(xn, wg_ref[:, f0:f0 + FFN_CHUNK])
        u = _dot(xn, wu_ref[:, f0:f0 + FFN_CHUNK])
        h = (a * jax.nn.sigmoid(a)) * u
        part = _dot(h.astype(BF16), wd_ref[f0:f0 + FFN_CHUNK, :])
        acc = part if acc is None else acc + part
    return x + 0.5 * acc


def _project_columns(xn, w_ref, o_ref, first_scale=None):
    for c0 in range(0, w_ref.shape[1], PROJ_CHUNK):
        p = _dot(xn, w_ref[:, c0:c0 + PROJ_CHUNK])
        if c0 == 0 and first_scale is not None:
            p = p * first_scale
        o_ref[:, c0:c0 + PROJ_CHUNK] = p.astype(o_ref.dtype)


def _ffn_sb_kernel(x_ref, g_ref, wg_ref, wu_ref, wd_ref, mg_ref, w_ref, o_ref, p_ref):
    y = _swiglu_half(x_ref[...], g_ref, wg_ref, wu_ref, wd_ref)
    o_ref[...] = y
    _project_columns(_rms(y, mg_ref[...]).astype(BF16), w_ref, p_ref, first_scale=SB_ZSCALE)


def _ffn_gla_kernel(x_ref, g_ref, wg_ref, wu_ref, wd_ref, mg_ref, w_ref, wlr_ref, wgk_ref,
                    bgk_ref, o_ref, p_ref, la_ref):
    y = _swiglu_half(x_ref[...], g_ref, wg_ref, wu_ref, wd_ref)
    o_ref[...] = y
    hn = _rms(y, mg_ref[...]).astype(BF16)
    lr = _dot(hn, wlr_ref[...])
    z = _dot(lr.astype(BF16), wgk_ref[...]) + bgk_ref[...]
    logsig = jnp.minimum(z, 0.0) - jnp.log1p(jnp.exp(-jnp.abs(z)))
    la_ref[...] = logsig / GLA_GATE_TAU
    _project_columns(hn, w_ref, p_ref)


def _ffn_out_kernel(x_ref, y_ref, wo_ref, g_ref, wg_ref, wu_ref, wd_ref, fg_ref, o_ref, *, final):
    x = x_ref[...] + _dot(y_ref[...], wo_ref[...])
    y = _swiglu_half(x, g_ref, wg_ref, wu_ref, wd_ref)
    if final:
        y = _rms(y, fg_ref[...])
    o_ref[...] = y


def _resident(a):
    return pl.BlockSpec(a.shape, lambda i: (0,) * a.ndim, pipeline_mode=pl.Buffered(1))


def _rows(tm, n):
    return pl.BlockSpec((tm, n), lambda i: (i, 0))


def _cast_rider_specs(riders, nsteps):
    in_specs, out_specs, out_shapes = [], [], []
    for src, layer in riders:
        _, r, c = src.shape
        nblk = max(n for n in (32, 16, 8, 4, 2, 1)
                   if nsteps % n == 0 and r % (n * BF16_SUBLANES) == 0)
        per = nsteps // nblk
        in_specs.append(pl.BlockSpec((None, r // nblk, c),
                                     lambda i, layer=layer, per=per: (layer, i // per, 0)))
        out_specs.append(pl.BlockSpec((r // nblk, c), lambda i, per=per: (i // per, 0)))
        out_shapes.append(jax.ShapeDtypeStruct((r, c), BF16))
    return in_specs, out_specs, out_shapes


def _with_cast_riders(body, n_in, n_rider):
    def kern(*refs):
        n_out = len(refs) - n_in - 2 * n_rider
        ins, rin = refs[:n_in], refs[n_in:n_in + n_rider]
        outs, rout = refs[n_in + n_rider:n_in + n_rider + n_out], refs[n_in + n_rider + n_out:]
        for src, dst in zip(rin, rout):
            dst[...] = src[...].astype(dst.dtype)
        body(*ins, *outs)
    return kern


def _ffn_call(body, name, x_like, consts, out_shapes, out_specs, riders, tm):
    m = x_like[0].shape[0]
    rin, rout, rshape = _cast_rider_specs(riders, m // tm)
    n_in = len(x_like) + len(consts)
    res = pl.pallas_call(
        _with_cast_riders(body, n_in, len(riders)),
        out_shape=tuple(out_shapes) + tuple(rshape),
        grid=(m // tm,),
        in_specs=[_rows(tm, a.shape[1]) for a in x_like] + [_resident(a) for a in consts] + rin,
        out_specs=tuple(out_specs) + tuple(rout),
        compiler_params=pltpu.CompilerParams(dimension_semantics=("parallel",)),
        name=name,
    )(*x_like, *consts, *[src for src, _ in riders])
    return res[:len(out_shapes)], res[len(out_shapes):]


def _ffn_in(x, ffn_w, mg, mix_w, riders, *, tm=FFN_ROWS):
    m, d = x.shape
    n = mix_w[0].shape[1]
    gla = len(mix_w) > 1
    out_shape = [jax.ShapeDtypeStruct((m, d), F32), jax.ShapeDtypeStruct((m, n), BF16)]
    out_specs = [_rows(tm, d), _rows(tm, n)]
    if gla:
        dk = mix_w[2].shape[1]
        out_shape.append(jax.ShapeDtypeStruct((m, dk), F32))
        out_specs.append(_rows(tm, dk))
    return _ffn_call(_ffn_gla_kernel if gla else _ffn_sb_kernel,
                     "ffn_gla_proj" if gla else "ffn_sb_proj",
                     (x,), (*ffn_w, mg, *mix_w), out_shape, out_specs, riders, tm)


def _ffn_out(x, y, wo, ffn_w, fg, riders, *, final, tm=2 * FFN_ROWS):
    m, d = x.shape
    (out,), cast = _ffn_call(functools.partial(_ffn_out_kernel, final=final), "out_proj_ffn",
                             (x, y), (wo, *ffn_w, fg), [jax.ShapeDtypeStruct((m, d), F32)],
                             [_rows(tm, d)], riders, tm)
    return out, cast


def _gla_consts():
    c = GLA_CHUNK
    r = np.arange(c)
    tmat = r[None, :] <= r[:, None]
    masks, signs = [], []
    for lvl in range(GLA_LEVELS):
        n = c >> lvl
        half = n // 2
        same = (r[:, None] // n) == (r[None, :] // n)
        masks.append(same & ((r[:, None] % n) >= half) & ((r[None, :] % n) < half))
        signs.append(np.where((r % n) >= half, LOG2E, -LOG2E))
    masks.append(r[:, None] == r[None, :])
    signs = np.broadcast_to(np.stack(signs, 0)[:, :, None], (GLA_LEVELS, c, LANES))
    return (tmat.astype(np.float32), np.stack(masks, 0).astype(np.float32),
            np.ascontiguousarray(signs, dtype=np.float32))


def _level_refs(b):
    c, n = b.shape
    sub = 8
    refs = []
    for lvl in range(GLA_LEVELS):
        blk = c >> lvl
        half = blk // 2
        if blk >= 2 * sub:
            refs.append(jnp.concatenate(
                [jnp.broadcast_to(b[m + half:m + half + 1], (blk, n)) for m in range(0, c, blk)], 0))
        else:
            b3 = b.reshape(c // sub, sub, n)
            row = lax.broadcasted_iota(jnp.int32, b3.shape, 1)
            ref = jnp.broadcast_to(b3[:, half:half + 1], b3.shape)
            for m in range(blk, sub, blk):
                ref = jnp.where(row >= m, jnp.broadcast_to(b3[:, m + half:m + half + 1], b3.shape), ref)
            refs.append(ref.reshape(c, n))
    return refs


def _gla_rows(q_ref, k_ref, v_ref, g_ref, la_ref, tm_ref, mask_ref, sg_ref, on_ref, st_ref, o_ref):
    c = GLA_CHUNK
    hk = q_ref.shape[1] // GLA_HEADS
    hv = v_ref.shape[1] // GLA_HEADS
    qscale = float(hk) ** -0.5
    tmat = tm_ref[...]
    onorm = on_ref[...]

    for c0 in range(0, q_ref.shape[0], c):
        sl = slice(c0, c0 + c)
        la = la_ref[sl, :]
        h1 = la.astype(BF16)
        r1 = la - h1.astype(F32)
        h2 = r1.astype(BF16)
        h3 = (r1 - h2.astype(F32)).astype(BF16)
        ball = _dot(tmat, h1) + _dot(tmat, h2) + _dot(tmat, h3)
        refs = _level_refs(ball)

        heads = range(GLA_HEADS)
        ksl = [slice(h * hk, (h + 1) * hk) for h in heads]
        vsl = [slice(h * hv, (h + 1) * hv) for h in heads]
        b = [ball[:, ksl[h]] for h in heads]
        q = [q_ref[sl, ksl[h]].astype(F32) * qscale for h in heads]
        k = [k_ref[sl, ksl[h]].astype(F32) for h in heads]
        v = [v_ref[sl, vsl[h]] for h in heads]
        st = [st_ref[h] for h in heads]

        scores = []
        for h in heads:
            s = mask_ref[GLA_LEVELS] * lax.dot_general(
                q[h].astype(BF16), k[h].astype(BF16), _NT, preferred_element_type=F32)
            for lvl in range(GLA_LEVELS):
                ref = refs[lvl][:, ksl[h]]
                f = jnp.exp2((b[h] - ref) * sg_ref[lvl])
                s = s + mask_ref[lvl] * lax.dot_general(
                    (q[h] * f).astype(BF16), (k[h] * f).astype(BF16), _NT,
                    preferred_element_type=F32)
            scores.append(s.astype(BF16))

        outs = []
        for h in heads:
            o = lax.dot_general((q[h] * jnp.exp(b[h])).astype(BF16), st[h].astype(BF16), _NT,
                                preferred_element_type=F32)
            outs.append(o + _dot(scores[h], v[h]))

        for h in heads:
            blast = b[h][c - 1:c, :]
            kd = (k[h] * jnp.exp(blast - b[h])).astype(BF16)
            st_ref[h] = jnp.exp(blast) * st[h] + lax.dot_general(
                v[h], kd, _TN, preferred_element_type=F32)

        for h in heads:
            g = g_ref[sl, vsl[h]].astype(F32)
            y = _rms(outs[h], onorm) * (g * jax.nn.sigmoid(g))
            o_ref[sl, vsl[h]] = y.astype(o_ref.dtype)


def _gla_kernel(q_ref, k_ref, v_ref, g_ref, la_ref, tm_ref, mask_ref, sg_ref, on_ref, o_ref,
                st_ref):
    @pl.when(pl.program_id(1) == 0)
    def _():
        st_ref[...] = jnp.zeros_like(st_ref)

    _gla_rows(q_ref, k_ref, v_ref, g_ref, la_ref, tm_ref, mask_ref, sg_ref, on_ref, st_ref, o_ref)


def _gla_mix(proj, la, onorm, batch, seq, *, ts=GLA_ROWS):
    m = proj.shape[0]
    dk = la.shape[1]
    dv = (proj.shape[1] - 2 * dk) // 2
    hk = dk // GLA_HEADS
    hv = dv // GLA_HEADS
    nsb = seq // ts
    tmat, masks, signs = _gla_consts()
    rows = lambda width, col: pl.BlockSpec((ts, width), lambda b, s: (b * nsb + s, col))
    consts = (jnp.asarray(tmat, BF16), jnp.asarray(masks), jnp.asarray(signs), onorm)
    return pl.pallas_call(
        _gla_kernel,
        out_shape=jax.ShapeDtypeStruct((m, dv), BF16),
        grid=(batch, nsb),
        in_specs=[
            rows(dk, 0), rows(dk, 1), rows(dv, (2 * dk) // dv), rows(dv, (2 * dk) // dv + 1),
            rows(dk, 0),
        ] + [pl.BlockSpec(a.shape, lambda b, s, nd=a.ndim: (0,) * nd) for a in consts],
        out_specs=rows(dv, 0),
        scratch_shapes=[pltpu.VMEM((GLA_HEADS, hv, hk), F32)],
        compiler_params=pltpu.CompilerParams(dimension_semantics=("parallel", "arbitrary")),
        name="gla_mix",
    )(proj, proj, proj, proj, la, *consts)


def _sb_kernel(q_ref, k_ref, v_ref, o_ref, acc_ref, *, tq, hd):
    i = pl.program_id(2)
    nh = acc_ref.shape[0]
    lane = lax.broadcasted_iota(jnp.int32, (tq, LANES), 1)
    cols = [slice((h // 2) * LANES, (h // 2 + 1) * LANES) for h in range(nh)]
    qh = []
    for h in range(nh):
        q = q_ref[:, cols[h]]
        qh.append(jnp.where(lane < hd if h % 2 == 0 else lane >= hd, q, jnp.zeros_like(q)))
    half = tq // 2
    rr = lax.broadcasted_iota(jnp.int32, (tq, tq), 0)
    cc = lax.broadcasted_iota(jnp.int32, (tq, tq), 1)
    suffix = (rr >= cc).astype(BF16)
    causal_up = (lax.broadcasted_iota(jnp.int32, (half, half), 1)
                 < lax.broadcasted_iota(jnp.int32, (half, half), 0))
    causal_lo = (lax.broadcasted_iota(jnp.int32, (half, tq), 1)
                 < lax.broadcasted_iota(jnp.int32, (half, tq), 0) + half)

    def softplus_stage(h, rows, start, nk, mask):
        z = lax.dot_general(qh[h][rows], k_ref[pl.ds(start, nk), cols[h]], _NT,
                            preferred_element_type=F32)
        sp = jnp.maximum(z, jnp.log(1.0 + jnp.exp2(jnp.minimum(z, SB_EXP2_CAP))) * LOG2E)
        if mask is not None:
            sp = jnp.where(mask, sp, 0.0)
        return z, sp.astype(BF16), jnp.sum(sp, axis=-1, keepdims=True)

    def weight_stage(z, rem, carry, mask):
        w = jnp.exp2(z - rem - carry)
        if mask is not None:
            w = jnp.where(mask, w, 0.0)
        return w.astype(BF16)

    def tiles(groups, carries):
        flat = [(h, rows, *job) for h, rows, jobs in groups for job in jobs]
        first = [softplus_stage(h, rows, start, nk, mask) for h, rows, start, nk, mask, _ in flat]
        rems = [_dot(first[n][1], suffix[:flat[n][3], :flat[n][3]]) for n in range(len(flat))]
        mass = [m for _, _, m in first]
        ws = [weight_stage(first[n][0], rems[n], carries(n, mass), flat[n][4])
              for n in range(len(flat))]
        pvs, n = [], 0
        for h, _, jobs in groups:
            vs = []
            for start, nk, _, keep in jobs:
                v = v_ref[pl.ds(start, nk), cols[h]]
                vs.append(v if keep is None else jnp.where(keep, v, jnp.zeros_like(v)))
            pvs.append(_dot(jnp.concatenate(ws[n:n + len(jobs)], axis=1),
                            jnp.concatenate(vs, axis=0)))
            n += len(jobs)
        return pvs, mass

    diag = pl.multiple_of(i * tq, tq)
    prev = pl.multiple_of(jnp.maximum(i - 1, 0) * tq, tq)
    before = (prev, tq, None, i > 0)
    groups = []
    for h in range(nh):
        groups.append((h, slice(0, half), [(diag, half, causal_up, None), before]))
        groups.append((h, slice(half, tq), [(diag, tq, causal_lo, None), before]))
    pv, mass = tiles(groups, lambda n, mass: mass[n - 1] if n % 2 else jnp.zeros((half, 1), F32))
    carry = []
    for h in range(nh):
        acc_ref[h, 0:half] = pv[2 * h]
        acc_ref[h, half:tq] = pv[2 * h + 1]
        carry.append(jnp.concatenate([mass[4 * h] + mass[4 * h + 1],
                                      mass[4 * h + 2] + mass[4 * h + 3]], axis=0))

    def lowest(cs):
        return functools.reduce(jnp.minimum, [jnp.min(c) for c in cs])

    def live(c):
        t, _, low = c
        return jnp.logical_and(t < i - 1, low < SB_DEAD_MASS_LOG2)

    def body(c):
        t, cs, _ = c
        start = pl.multiple_of((i - 2 - t) * tq, tq)
        pv, mass = tiles([(h, slice(0, tq), [(start, tq, None, None)]) for h in range(nh)],
                         lambda n, mass: cs[n])
        for h in range(nh):
            acc_ref[h] += pv[h]
        cs = tuple(cs[h] + mass[h] for h in range(nh))
        return t + 1, cs, lowest(cs)

    lax.while_loop(live, body, (jnp.int32(0), tuple(carry), lowest(carry)))
    for p in range(nh // 2):
        o_ref[:, cols[2 * p]] = jnp.where(
            lane < hd, acc_ref[2 * p], acc_ref[2 * p + 1]).astype(o_ref.dtype)


def _sb_mix(qkv, batch, seq, *, tq=256, pairs=SB_PAIRS_PER_STEP):
    m = qkv.shape[0]
    d = qkv.shape[1] // 3
    hd = d // SB_HEADS
    ngrp = d // (pairs * LANES)
    nq = seq // tq
    width = pairs * LANES
    assert hd == SB_HEAD_DIM and d == PROJ_CHUNK
    kv_spec = lambda col0: pl.BlockSpec((seq, width), lambda b, p, i: (b, col0 + p),
                                        pipeline_mode=pl.Buffered(1))
    return pl.pallas_call(
        functools.partial(_sb_kernel, tq=tq, hd=hd),
        out_shape=jax.ShapeDtypeStruct((m, d), BF16),
        grid=(batch, ngrp, nq),
        in_specs=[
            pl.BlockSpec((tq, width), lambda b, p, i: (b * nq + i, p)),
            kv_spec(ngrp),
            kv_spec(2 * ngrp),
        ],
        out_specs=pl.BlockSpec((tq, width), lambda b, p, i: (b * nq + i, p)),
        scratch_shapes=[pltpu.VMEM((2 * pairs, tq, LANES), F32)],
        compiler_params=pltpu.CompilerParams(
            dimension_semantics=("parallel", "parallel", "arbitrary")),
        name="sb_mix",
    )(qkv, qkv, qkv)


def kernel(x, ffn1_norm, ffn1_w_gate, ffn1_w_up, ffn1_w_down, mix_norm, ffn2_norm, ffn2_w_gate, ffn2_w_up, ffn2_w_down, gla_w_in, gla_w_gk2, gla_b_gk, gla_o_norm, gla_w_out, sb_w_in, sb_w_out, final_norm):
    batch, seq, d = x.shape
    depth = ffn1_norm.shape[0]
    dk = gla_w_gk2.shape[2]
    n_main = gla_w_in.shape[2] - GLA_GATE_RANK
    xs = x.reshape(batch * seq, d)
    fg = final_norm.reshape(1, d)
    ffn1_w = tuple(w[0].astype(BF16) for w in (ffn1_w_gate, ffn1_w_up, ffn1_w_down))
    gla_in = gla_w_in.astype(BF16)
    mix_in = gla_in[0]
    for i in range(depth):
        j = i // 2
        gla = i % 2 == 0
        mg = mix_norm[i].reshape(1, d)
        ffn1 = (ffn1_norm[i].reshape(1, d), *ffn1_w)
        riders = [(ffn2_w_gate, i), (ffn2_w_up, i), (ffn2_w_down, i),
                  (gla_w_out if gla else sb_w_out, j)]
        if gla:
            wlr = jnp.pad(mix_in[:, n_main:], ((0, 0), (0, LANES - GLA_GATE_RANK)))
            wgk = jnp.pad(gla_w_gk2[j], ((0, LANES - GLA_GATE_RANK), (0, 0))).astype(BF16)
            (xs, proj, la), cast = _ffn_in(
                xs, ffn1, mg, (mix_in[:, :n_main], wlr, wgk, gla_b_gk[j].reshape(1, dk)), riders)
            y = _gla_mix(proj, la, gla_o_norm[j].reshape(1, -1), batch, seq)
        else:
            (xs, qkv), cast = _ffn_in(xs, ffn1, mg, (mix_in,), riders)
            y = _sb_mix(qkv, batch, seq)
        ffn2 = (ffn2_norm[i].reshape(1, d), *cast[:3])
        riders = []
        if i + 1 < depth:
            riders = [(ffn1_w_gate, i + 1), (ffn1_w_up, i + 1), (ffn1_w_down, i + 1)]
            if gla:
                riders.append((sb_w_in, (i + 1) // 2))
        xs, cast_next = _ffn_out(xs, y, cast[3], ffn2, fg, riders, final=(i == depth - 1))
        if riders:
            ffn1_w = cast_next[:3]
            mix_in = cast_next[3] if gla else gla_in[(i + 1) // 2]
    return xs.reshape(batch, seq, d)
```

```python
import functools

import numpy as np
import jax
import jax.numpy as jnp
from jax import lax
from jax.experimental import pallas as pl
from jax.experimental.pallas import tpu as pltpu

F32 = jnp.float32
BF16 = jnp.bfloat16

EPS = 1e-6
GLA_HEADS = 4
GLA_GATE_RANK = 16
GLA_GATE_TAU = 16.0
GLA_CHUNK = 64
GLA_LEVELS = 6
GLA_ROWS = 512
SB_HEADS = 16
SB_EXP2_CAP = 64.0
SB_PAIRS_PER_STEP = 4
LOG2E = 1.4426950408889634
SB_HEAD_DIM = 64
SB_ZSCALE = SB_HEAD_DIM ** -0.5 * LOG2E
SB_DEAD_MASS_LOG2 = 152.0
LANES = 128
BF16_SUBLANES = 16
MXU_DIM = 256
FFN_ROWS = 512
FFN_CHUNK = MXU_DIM
PROJ_CHUNK = 4 * MXU_DIM

_NT = (((1,), (1,)), ((), ()))
_TN = (((0,), (0,)), ((), ()))


def _rms(x, g):
    ms = jnp.mean(x * x, axis=-1, keepdims=True)
    return x * lax.rsqrt(ms + EPS) * g


def _dot(a, b):
    return jnp.dot(a, b, preferred_element_type=F32)


def _swiglu_half(x, g_ref, wg_ref, wu_ref, wd_ref):
    xn = _rms(x, g_ref[...]).astype(BF16)
    acc = None
    for f0 in range(0, wg_ref.shape[1], FFN_CHUNK):
        a = _dot(xn, wg_ref[:, f0:f0 + FFN_CHUNK])
        u = _dot(xn, wu_ref[:, f0:f0 + FFN_CHUNK])
        h = (a * jax.nn.sigmoid(a)) * u
        part = _dot(h.astype(BF16), wd_ref[f0:f0 + FFN_CHUNK, :])
        acc = part if acc is None else acc + part
    return x + 0.5 * acc


def _project_columns(xn, w_ref, o_ref, first_scale=None):
    for c0 in range(0, w_ref.shape[1], PROJ_CHUNK):
        p = _dot(xn, w_ref[:, c0:c0 + PROJ_CHUNK])
        if c0 == 0 and first_scale is not None:
            p = p * first_scale
        o_ref[:, c0:c0 + PROJ_CHUNK] = p.astype(o_ref.dtype)


def _ffn_sb_kernel(x_ref, g_ref, wg_ref, wu_ref, wd_ref, mg_ref, w_ref, o_ref, p_ref):
    y = _swiglu_half(x_ref[...], g_ref, wg_ref, wu_ref, wd_ref)
    o_ref[...] = y
    _project_columns(_rms(y, mg_ref[...]).astype(BF16), w_ref, p_ref, first_scale=SB_ZSCALE)


def _ffn_gla_kernel(x_ref, g_ref, wg_ref, wu_ref, wd_ref, mg_ref, w_ref, wlr_ref, wgk_ref,
                    bgk_ref, o_ref, p_ref, la_ref):
    y = _swiglu_half(x_ref[...], g_ref, wg_ref, wu_ref, wd_ref)
    o_ref[...] = y
    hn = _rms(y, mg_ref[...]).astype(BF16)
    lr = _dot(hn, wlr_ref[...])
    z = _dot(lr.astype(BF16), wgk_ref[...]) + bgk_ref[...]
    logsig = jnp.minimum(z, 0.0) - jnp.log1p(jnp.exp(-jnp.abs(z)))
    la_ref[...] = logsig / GLA_GATE_TAU
    _project_columns(hn, w_ref, p_ref)


def _ffn_out_kernel(x_ref, y_ref, wo_ref, g_ref, wg_ref, wu_ref, wd_ref, fg_ref, o_ref, *, final):
    x = x_ref[...] + _dot(y_ref[...], wo_ref[...])
    y = _swiglu_half(x, g_ref, wg_ref, wu_ref, wd_ref)
    if final:
        y = _rms(y, fg_ref[...])
    o_ref[...] = y


def _resident(a):
    return pl.BlockSpec(a.shape, lambda i: (0,) * a.ndim, pipeline_mode=pl.Buffered(1))


def _rows(tm, n):
    return pl.BlockSpec((tm, n), lambda i: (i, 0))


def _cast_rider_specs(riders, nsteps):
    in_specs, out_specs, out_shapes = [], [], []
    for src, layer in riders:
        _, r, c = src.shape
        nblk = max(n for n in (32, 16, 8, 4, 2, 1)
                   if nsteps % n == 0 and r % (n * BF16_SUBLANES) == 0)
        per = nsteps // nblk
        in_specs.append(pl.BlockSpec((None, r // nblk, c),
                                     lambda i, layer=layer, per=per: (layer, i // per, 0)))
        out_specs.append(pl.BlockSpec((r // nblk, c), lambda i, per=per: (i // per, 0)))
        out_shapes.append(jax.ShapeDtypeStruct((r, c), BF16))
    return in_specs, out_specs, out_shapes


def _with_cast_riders(body, n_in, n_rider):
    def kern(*refs):
        n_out = len(refs) - n_in - 2 * n_rider
        ins, rin = refs[:n_in], refs[n_in:n_in + n_rider]
        outs, rout = refs[n_in + n_rider:n_in + n_rider + n_out], refs[n_in + n_rider + n_out:]
        for src, dst in zip(rin, rout):
            dst[...] = src[...].astype(dst.dtype)
        body(*ins, *outs)
    return kern


def _ffn_call(body, name, x_like, consts, out_shapes, out_specs, riders, tm):
    m = x_like[0].shape[0]
    rin, rout, rshape = _cast_rider_specs(riders, m // tm)
    n_in = len(x_like) + len(consts)
    res = pl.pallas_call(
        _with_cast_riders(body, n_in, len(riders)),
        out_shape=tuple(out_shapes) + tuple(rshape),
        grid=(m // tm,),
        in_specs=[_rows(tm, a.shape[1]) for a in x_like] + [_resident(a) for a in consts] + rin,
        out_specs=tuple(out_specs) + tuple(rout),
        compiler_params=pltpu.CompilerParams(dimension_semantics=("arbitrary",)),
        name=name,
    )(*x_like, *consts, *[src for src, _ in riders])
    return res[:len(out_shapes)], res[len(out_shapes):]


def _ffn_in(x, ffn_w, mg, mix_w, riders, *, tm=FFN_ROWS):
    m, d = x.shape
    n = mix_w[0].shape[1]
    gla = len(mix_w) > 1
    out_shape = [jax.ShapeDtypeStruct((m, d), F32), jax.ShapeDtypeStruct((m, n), BF16)]
    out_specs = [_rows(tm, d), _rows(tm, n)]
    if gla:
        dk = mix_w[2].shape[1]
        out_shape.append(jax.ShapeDtypeStruct((m, dk), F32))
        out_specs.append(_rows(tm, dk))
    return _ffn_call(_ffn_gla_kernel if gla else _ffn_sb_kernel,
                     "ffn_gla_proj" if gla else "ffn_sb_proj",
                     (x,), (*ffn_w, mg, *mix_w), out_shape, out_specs, riders, tm)


def _ffn_out(x, y, wo, ffn_w, fg, riders, *, final, tm=2 * FFN_ROWS):
    m, d = x.shape
    (out,), cast = _ffn_call(functools.partial(_ffn_out_kernel, final=final), "out_proj_ffn",
                             (x, y), (wo, *ffn_w, fg), [jax.ShapeDtypeStruct((m, d), F32)],
                             [_rows(tm, d)], riders, tm)
    return out, cast


def _gla_consts():
    c = GLA_CHUNK
    r = np.arange(c)
    tmat = r[None, :] <= r[:, None]
    masks, signs = [], []
    for lvl in range(GLA_LEVELS):
        n = c >> lvl
        half = n // 2
        same = (r[:, None] // n) == (r[None, :] // n)
        masks.append(same & ((r[:, None] % n) >= half) & ((r[None, :] % n) < half))
        signs.append(np.where((r % n) >= half, LOG2E, -LOG2E))
    masks.append(r[:, None] == r[None, :])
    signs = np.broadcast_to(np.stack(signs, 0)[:, :, None], (GLA_LEVELS, c, LANES))
    return (tmat.astype(np.float32), np.stack(masks, 0).astype(np.float32),
            np.ascontiguousarray(signs, dtype=np.float32))


def _level_refs(b):
    c, n = b.shape
    sub = 8
    refs = []
    for lvl in range(GLA_LEVELS):
        blk = c >> lvl
        half = blk // 2
        if blk >= 2 * sub:
            refs.append(jnp.concatenate(
                [jnp.broadcast_to(b[m + half:m + half + 1], (blk, n)) for m in range(0, c, blk)], 0))
        else:
            b3 = b.reshape(c // sub, sub, n)
            row = lax.broadcasted_iota(jnp.int32, b3.shape, 1)
            ref = jnp.broadcast_to(b3[:, half:half + 1], b3.shape)
            for m in range(blk, sub, blk):
                ref = jnp.where(row >= m, jnp.broadcast_to(b3[:, m + half:m + half + 1], b3.shape), ref)
            refs.append(ref.reshape(c, n))
    return refs


def _gla_rows(q_ref, k_ref, v_ref, g_ref, la_ref, tm_ref, mask_ref, sg_ref, on_ref, st_ref, o_ref):
    c = GLA_CHUNK
    hk = q_ref.shape[1] // GLA_HEADS
    hv = v_ref.shape[1] // GLA_HEADS
    qscale = float(hk) ** -0.5
    tmat = tm_ref[...]
    onorm = on_ref[...]

    for c0 in range(0, q_ref.shape[0], c):
        sl = slice(c0, c0 + c)
        la = la_ref[sl, :]
        h1 = la.astype(BF16)
        r1 = la - h1.astype(F32)
        h2 = r1.astype(BF16)
        h3 = (r1 - h2.astype(F32)).astype(BF16)
        ball = _dot(tmat, h1) + _dot(tmat, h2) + _dot(tmat, h3)
        refs = _level_refs(ball)

        heads = range(GLA_HEADS)
        ksl = [slice(h * hk, (h + 1) * hk) for h in heads]
        vsl = [slice(h * hv, (h + 1) * hv) for h in heads]
        b = [ball[:, ksl[h]] for h in heads]
        q = [q_ref[sl, ksl[h]].astype(F32) * qscale for h in heads]
        k = [k_ref[sl, ksl[h]].astype(F32) for h in heads]
        v = [v_ref[sl, vsl[h]] for h in heads]
        st = [st_ref[h] for h in heads]

        scores = []
        for h in heads:
            s = mask_ref[GLA_LEVELS] * lax.dot_general(
                q[h].astype(BF16), k[h].astype(BF16), _NT, preferred_element_type=F32)
            for lvl in range(GLA_LEVELS):
                ref = refs[lvl][:, ksl[h]]
                f = jnp.exp2((b[h] - ref) * sg_ref[lvl])
                s = s + mask_ref[lvl] * lax.dot_general(
                    (q[h] * f).astype(BF16), (k[h] * f).astype(BF16), _NT,
                    preferred_element_type=F32)
            scores.append(s.astype(BF16))

        outs = []
        for h in heads:
            o = lax.dot_general((q[h] * jnp.exp(b[h])).astype(BF16), st[h].astype(BF16), _NT,
                                preferred_element_type=F32)
            outs.append(o + _dot(scores[h], v[h]))

        for h in heads:
            blast = b[h][c - 1:c, :]
            kd = (k[h] * jnp.exp(blast - b[h])).astype(BF16)
            st_ref[h] = jnp.exp(blast) * st[h] + lax.dot_general(
                v[h], kd, _TN, preferred_element_type=F32)

        for h in heads:
            g = g_ref[sl, vsl[h]].astype(F32)
            y = _rms(outs[h], onorm) * (g * jax.nn.sigmoid(g))
            o_ref[sl, vsl[h]] = y.astype(o_ref.dtype)


def _gla_kernel(q_ref, k_ref, v_ref, g_ref, la_ref, tm_ref, mask_ref, sg_ref, on_ref, o_ref,
                st_ref):
    @pl.when(pl.program_id(1) == 0)
    def _():
        st_ref[...] = jnp.zeros_like(st_ref)

    _gla_rows(q_ref, k_ref, v_ref, g_ref, la_ref, tm_ref, mask_ref, sg_ref, on_ref, st_ref, o_ref)


def _gla_mix(proj, la, onorm, batch, seq, *, ts=GLA_ROWS):
    m = proj.shape[0]
    dk = la.shape[1]
    dv = (proj.shape[1] - 2 * dk) // 2
    hk = dk // GLA_HEADS
    hv = dv // GLA_HEADS
    nsb = seq // ts
    tmat, masks, signs = _gla_consts()
    rows = lambda width, col: pl.BlockSpec((ts, width), lambda b, s: (b * nsb + s, col))
    consts = (jnp.asarray(tmat, BF16), jnp.asarray(masks), jnp.asarray(signs), onorm)
    return pl.pallas_call(
        _gla_kernel,
        out_shape=jax.ShapeDtypeStruct((m, dv), BF16),
        grid=(batch, nsb),
        in_specs=[
            rows(dk, 0), rows(dk, 1), rows(dv, (2 * dk) // dv), rows(dv, (2 * dk) // dv + 1),
            rows(dk, 0),
        ] + [pl.BlockSpec(a.shape, lambda b, s, nd=a.ndim: (0,) * nd) for a in consts],
        out_specs=rows(dv, 0),
        scratch_shapes=[pltpu.VMEM((GLA_HEADS, hv, hk), F32)],
        compiler_params=pltpu.CompilerParams(dimension_semantics=("parallel", "arbitrary")),
        name="gla_mix",
    )(proj, proj, proj, proj, la, *consts)


def _sb_kernel(q_ref, k_ref, v_ref, o_ref, acc_ref, *, tq, hd):
    i = pl.program_id(2)
    nh = acc_ref.shape[0]
    lane = lax.broadcasted_iota(jnp.int32, (tq, LANES), 1)
    cols = [slice((h // 2) * LANES, (h // 2 + 1) * LANES) for h in range(nh)]
    qh = []
    for h in range(nh):
        q = q_ref[:, cols[h]]
        qh.append(jnp.where(lane < hd if h % 2 == 0 else lane >= hd, q, jnp.zeros_like(q)))
    half = tq // 2
    rr = lax.broadcasted_iota(jnp.int32, (tq, tq), 0)
    cc = lax.broadcasted_iota(jnp.int32, (tq, tq), 1)
    suffix = (rr >= cc).astype(BF16)
    causal_up = (lax.broadcasted_iota(jnp.int32, (half, half), 1)
                 < lax.broadcasted_iota(jnp.int32, (half, half), 0))
    causal_lo = (lax.broadcasted_iota(jnp.int32, (half, tq), 1)
                 < lax.broadcasted_iota(jnp.int32, (half, tq), 0) + half)

    def softplus_stage(h, rows, start, nk, mask):
        z = lax.dot_general(qh[h][rows], k_ref[pl.ds(start, nk), cols[h]], _NT,
                            preferred_element_type=F32)
        sp = jnp.maximum(z, jnp.log(1.0 + jnp.exp2(jnp.minimum(z, SB_EXP2_CAP))) * LOG2E)
        if mask is not None:
            sp = jnp.where(mask, sp, 0.0)
        return z, sp.astype(BF16), jnp.sum(sp, axis=-1, keepdims=True)

    def weight_stage(z, rem, carry, mask):
        w = jnp.exp2(z - rem - carry)
        if mask is not None:
            w = jnp.where(mask, w, 0.0)
        return w.astype(BF16)

    def tiles(groups, carries):
        flat = [(h, rows, *job) for h, rows, jobs in groups for job in jobs]
        first = [softplus_stage(h, rows, start, nk, mask) for h, rows, start, nk, mask, _ in flat]
        rems = [_dot(first[n][1], suffix[:flat[n][3], :flat[n][3]]) for n in range(len(flat))]
        mass = [m for _, _, m in first]
        ws = [weight_stage(first[n][0], rems[n], carries(n, mass), flat[n][4])
              for n in range(len(flat))]
        pvs, n = [], 0
        for h, _, jobs in groups:
            vs = []
            for start, nk, _, keep in jobs:
                v = v_ref[pl.ds(start, nk), cols[h]]
                vs.append(v if keep is None else jnp.where(keep, v, jnp.zeros_like(v)))
            pvs.append(_dot(jnp.concatenate(ws[n:n + len(jobs)], axis=1),
                            jnp.concatenate(vs, axis=0)))
            n += len(jobs)
        return pvs, mass

    diag = pl.multiple_of(i * tq, tq)
    prev = pl.multiple_of(jnp.maximum(i - 1, 0) * tq, tq)
    before = (prev, tq, None, i > 0)
    groups = []
    for h in range(nh):
        groups.append((h, slice(0, half), [(diag, half, causal_up, None), before]))
        groups.append((h, slice(half, tq), [(diag, tq, causal_lo, None), before]))
    pv, mass = tiles(groups, lambda n, mass: mass[n - 1] if n % 2 else jnp.zeros((half, 1), F32))
    carry = []
    for h in range(nh):
        acc_ref[h, 0:half] = pv[2 * h]
        acc_ref[h, half:tq] = pv[2 * h + 1]
        carry.append(jnp.concatenate([mass[4 * h] + mass[4 * h + 1],
                                      mass[4 * h + 2] + mass[4 * h + 3]], axis=0))

    def lowest(cs):
        return functools.reduce(jnp.minimum, [jnp.min(c) for c in cs])

    def live(c):
        t, _, low = c
        return jnp.logical_and(t < i - 1, low < SB_DEAD_MASS_LOG2)

    def body(c):
        t, cs, _ = c
        start = pl.multiple_of((i - 2 - t) * tq, tq)
        pv, mass = tiles([(h, slice(0, tq), [(start, tq, None, None)]) for h in range(nh)],
                         lambda n, mass: cs[n])
        for h in range(nh):
            acc_ref[h] += pv[h]
        cs = tuple(cs[h] + mass[h] for h in range(nh))
        return t + 1, cs, lowest(cs)

    lax.while_loop(live, body, (jnp.int32(0), tuple(carry), lowest(carry)))
    for p in range(nh // 2):
        o_ref[:, cols[2 * p]] = jnp.where(
            lane < hd, acc_ref[2 * p], acc_ref[2 * p + 1]).astype(o_ref.dtype)


def _sb_mix(qkv, batch, seq, *, tq=256, pairs=SB_PAIRS_PER_STEP):
    m = qkv.shape[0]
    d = qkv.shape[1] // 3
    hd = d // SB_HEADS
    ngrp = d // (pairs * LANES)
    nq = seq // tq
    width = pairs * LANES
    assert hd == SB_HEAD_DIM and d == PROJ_CHUNK
    kv_spec = lambda col0: pl.BlockSpec((seq, width), lambda b, p, i: (b, col0 + p))
    return pl.pallas_call(
        functools.partial(_sb_kernel, tq=tq, hd=hd),
        out_shape=jax.ShapeDtypeStruct((m, d), BF16),
        grid=(batch, ngrp, nq),
        in_specs=[
            pl.BlockSpec((tq, width), lambda b, p, i: (b * nq + i, p)),
            kv_spec(ngrp),
            kv_spec(2 * ngrp),
        ],
        out_specs=pl.BlockSpec((tq, width), lambda b, p, i: (b * nq + i, p)),
        scratch_shapes=[pltpu.VMEM((2 * pairs, tq, LANES), F32)],
        compiler_params=pltpu.CompilerParams(
            dimension_semantics=("parallel", "parallel", "arbitrary")),
        name="sb_mix",
    )(qkv, qkv, qkv)


def kernel(x, ffn1_norm, ffn1_w_gate, ffn1_w_up, ffn1_w_down, mix_norm, ffn2_norm, ffn2_w_gate, ffn2_w_up, ffn2_w_down, gla_w_in, gla_w_gk2, gla_b_gk, gla_o_norm, gla_w_out, sb_w_in, sb_w_out, final_norm):
    batch, seq, d = x.shape
    depth = ffn1_norm.shape[0]
    dk = gla_w_gk2.shape[2]
    n_main = gla_w_in.shape[2] - GLA_GATE_RANK
    xs = x.reshape(batch * seq, d)
    fg = final_norm.reshape(1, d)
    ffn1_w = tuple(w[0].astype(BF16) for w in (ffn1_w_gate, ffn1_w_up, ffn1_w_down))
    gla_in = gla_w_in.astype(BF16)
    mix_in = gla_in[0]
    for i in range(depth):
        j = i // 2
        gla = i % 2 == 0
        mg = mix_norm[i].reshape(1, d)
        ffn1 = (ffn1_norm[i].reshape(1, d), *ffn1_w)
        riders = [(ffn2_w_gate, i), (ffn2_w_up, i), (ffn2_w_down, i),
                  (gla_w_out if gla else sb_w_out, j)]
        if gla:
            wlr = jnp.pad(mix_in[:, n_main:], ((0, 0), (0, LANES - GLA_GATE_RANK)))
            wgk = jnp.pad(gla_w_gk2[j], ((0, LANES - GLA_GATE_RANK), (0, 0))).astype(BF16)
            (xs, proj, la), cast = _ffn_in(
                xs, ffn1, mg, (mix_in[:, :n_main], wlr, wgk, gla_b_gk[j].reshape(1, dk)), riders)
            y = _gla_mix(proj, la, gla_o_norm[j].reshape(1, -1), batch, seq)
        else:
            (xs, qkv), cast = _ffn_in(xs, ffn1, mg, (mix_in,), riders)
            y = _sb_mix(qkv, batch, seq)
        ffn2 = (ffn2_norm[i].reshape(1, d), *cast[:3])
        riders = []
        if i + 1 < depth:
            riders = [(ffn1_w_gate, i + 1), (ffn1_w_up, i + 1), (ffn1_w_down, i + 1)]
            if gla:
                riders.append((sb_w_in, (i + 1) // 2))
        xs, cast_next = _ffn_out(xs, y, cast[3], ffn2, fg, riders, final=(i == depth - 1))
        if riders:
            ffn1_w = cast_next[:3]
            mix_in = cast_next[3] if gla else gla_in[(i + 1) // 2]
    return xs.reshape(batch, seq, d)
```

```python
import functools

import numpy as np
import jax
import jax.numpy as jnp
from jax import lax
from jax.experimental import pallas as pl
from jax.experimental.pallas import tpu as pltpu

F32 = jnp.float32
BF16 = jnp.bfloat16

EPS = 1e-6
GLA_HEADS = 4
GLA_GATE_RANK = 16
GLA_GATE_TAU = 16.0
GLA_CHUNK = 64
GLA_LEVELS = 6
GLA_ROWS = 512
SB_HEADS = 16
SB_EXP2_CAP = 64.0
SB_PAIRS_PER_STEP = 4
LOG2E = 1.4426950408889634
SB_HEAD_DIM = 64
SB_ZSCALE = SB_HEAD_DIM ** -0.5 * LOG2E
SB_DEAD_MASS_LOG2 = 152.0
LANES = 128
BF16_SUBLANES = 16
MXU_DIM = 256
FFN_ROWS = 512
FFN_CHUNK = MXU_DIM
PROJ_CHUNK = 4 * MXU_DIM

_NT = (((1,), (1,)), ((), ()))
_TN = (((0,), (0,)), ((), ()))


def _rms(x, g):
    ms = jnp.mean(x * x, axis=-1, keepdims=True)
    return x * lax.rsqrt(ms + EPS) * g


def _dot(a, b):
    return jnp.dot(a, b, preferred_element_type=F32)


def _swiglu_half(x, g_ref, wg_ref, wu_ref, wd_ref):
    xn = _rms(x, g_ref[...]).astype(BF16)
    acc = None
    for f0 in range(0, wg_ref.shape[1], FFN_CHUNK):
        a = _dot(xn, wg_ref[:, f0:f0 + FFN_CHUNK])
        u = _dot(xn, wu_ref[:, f0:f0 + FFN_CHUNK])
        h = (a * jax.nn.sigmoid(a)) * u
        part = _dot(h.astype(BF16), wd_ref[f0:f0 + FFN_CHUNK, :])
        acc = part if acc is None else acc + part
    return x + 0.5 * acc


def _project_columns(xn, w_ref, o_ref, first_scale=None, between=()):
    between = list(between)
    for c0 in range(0, w_ref.shape[1], PROJ_CHUNK):
        p = _dot(xn, w_ref[:, c0:c0 + PROJ_CHUNK])
        if c0 == 0 and first_scale is not None:
            p = p * first_scale
        o_ref[:, c0:c0 + PROJ_CHUNK] = p.astype(o_ref.dtype)
        if between:
            between.pop(0)()


def _ffn_sb_kernel(x_ref, g_ref, wg_ref, wu_ref, wd_ref, mg_ref, w_ref, o_ref, p_ref):
    y = _swiglu_half(x_ref[...], g_ref, wg_ref, wu_ref, wd_ref)
    o_ref[...] = y
    _project_columns(_rms(y, mg_ref[...]).astype(BF16), w_ref, p_ref, first_scale=SB_ZSCALE)


def _ffn_gla_kernel(x_ref, g_ref, wg_ref, wu_ref, wd_ref, mg_ref, w_ref, wlr_ref, wgk_ref,
                    bgk_ref, o_ref, p_ref, la_ref):
    y = _swiglu_half(x_ref[...], g_ref, wg_ref, wu_ref, wd_ref)
    o_ref[...] = y
    hn = _rms(y, mg_ref[...]).astype(BF16)
    lr = _dot(hn, wlr_ref[...])
    stage = {}

    def gate_logits():
        stage["z"] = _dot(lr.astype(BF16), wgk_ref[...]) + bgk_ref[...]

    def gate_store():
        z = stage["z"]
        logsig = jnp.minimum(z, 0.0) - jnp.log(1.0 + jnp.exp(-jnp.abs(z)))
        la_ref[...] = logsig * (1.0 / GLA_GATE_TAU)

    _project_columns(hn, w_ref, p_ref, between=(gate_logits, gate_store))


def _ffn_out_kernel(x_ref, y_ref, wo_ref, g_ref, wg_ref, wu_ref, wd_ref, fg_ref, o_ref, *, final):
    x = x_ref[...] + _dot(y_ref[...], wo_ref[...])
    y = _swiglu_half(x, g_ref, wg_ref, wu_ref, wd_ref)
    if final:
        y = _rms(y, fg_ref[...])
    o_ref[...] = y


def _resident(a):
    return pl.BlockSpec(a.shape, lambda i: (0,) * a.ndim, pipeline_mode=pl.Buffered(1))


def _rows(tm, n):
    return pl.BlockSpec((tm, n), lambda i: (i, 0))


def _cast_rider_specs(riders, nsteps):
    in_specs, out_specs, out_shapes = [], [], []
    for src, layer in riders:
        _, r, c = src.shape
        nblk = max(n for n in (32, 16, 8, 4, 2, 1)
                   if nsteps % n == 0 and r % (n * BF16_SUBLANES) == 0)
        per = nsteps // nblk
        in_specs.append(pl.BlockSpec((None, r // nblk, c),
                                     lambda i, layer=layer, per=per: (layer, i // per, 0)))
        out_specs.append(pl.BlockSpec((r // nblk, c), lambda i, per=per: (i // per, 0)))
        out_shapes.append(jax.ShapeDtypeStruct((r, c), BF16))
    return in_specs, out_specs, out_shapes


def _with_cast_riders(body, n_in, n_rider):
    def kern(*refs):
        n_out = len(refs) - n_in - 2 * n_rider
        ins, rin = refs[:n_in], refs[n_in:n_in + n_rider]
        outs, rout = refs[n_in + n_rider:n_in + n_rider + n_out], refs[n_in + n_rider + n_out:]
        for src, dst in zip(rin, rout):
            dst[...] = src[...].astype(dst.dtype)
        body(*ins, *outs)
    return kern


def _ffn_call(body, name, x_like, consts, out_shapes, out_specs, riders, tm):
    m = x_like[0].shape[0]
    rin, rout, rshape = _cast_rider_specs(riders, m // tm)
    n_in = len(x_like) + len(consts)
    res = pl.pallas_call(
        _with_cast_riders(body, n_in, len(riders)),
        out_shape=tuple(out_shapes) + tuple(rshape),
        grid=(m // tm,),
        in_specs=[_rows(tm, a.shape[1]) for a in x_like] + [_resident(a) for a in consts] + rin,
        out_specs=tuple(out_specs) + tuple(rout),
        compiler_params=pltpu.CompilerParams(dimension_semantics=("arbitrary",)),
        name=name,
    )(*x_like, *consts, *[src for src, _ in riders])
    return res[:len(out_shapes)], res[len(out_shapes):]


def _ffn_in(x, ffn_w, mg, mix_w, riders, *, tm=FFN_ROWS):
    m, d = x.shape
    n = mix_w[0].shape[1]
    gla = len(mix_w) > 1
    out_shape = [jax.ShapeDtypeStruct((m, d), F32), jax.ShapeDtypeStruct((m, n), BF16)]
    out_specs = [_rows(tm, d), _rows(tm, n)]
    if gla:
        dk = mix_w[2].shape[1]
        out_shape.append(jax.ShapeDtypeStruct((m, dk), F32))
        out_specs.append(_rows(tm, dk))
    return _ffn_call(_ffn_gla_kernel if gla else _ffn_sb_kernel,
                     "ffn_gla_proj" if gla else "ffn_sb_proj",
                     (x,), (*ffn_w, mg, *mix_w), out_shape, out_specs, riders, tm)


def _ffn_out(x, y, wo, ffn_w, fg, riders, *, final, tm=2 * FFN_ROWS):
    m, d = x.shape
    (out,), cast = _ffn_call(functools.partial(_ffn_out_kernel, final=final), "out_proj_ffn",
                             (x, y), (wo, *ffn_w, fg), [jax.ShapeDtypeStruct((m, d), F32)],
                             [_rows(tm, d)], riders, tm)
    return out, cast


def _gla_consts():
    c = GLA_CHUNK
    r = np.arange(c)
    tmat = r[None, :] <= r[:, None]
    masks, signs = [], []
    for lvl in range(GLA_LEVELS):
        n = c >> lvl
        half = n // 2
        same = (r[:, None] // n) == (r[None, :] // n)
        masks.append(same & ((r[:, None] % n) >= half) & ((r[None, :] % n) < half))
        signs.append(np.where((r % n) >= half, LOG2E, -LOG2E))
    masks.append(r[:, None] == r[None, :])
    signs = np.broadcast_to(np.stack(signs, 0)[:, :, None], (GLA_LEVELS, c, LANES))
    return (tmat.astype(np.float32), np.stack(masks, 0).astype(np.float32),
            np.ascontiguousarray(signs, dtype=np.float32))


def _level_refs(b):
    c, n = b.shape
    sub = 8
    refs = []
    for lvl in range(GLA_LEVELS):
        blk = c >> lvl
        half = blk // 2
        if blk >= 2 * sub:
            refs.append(jnp.concatenate(
                [jnp.broadcast_to(b[m + half:m + half + 1], (blk, n)) for m in range(0, c, blk)], 0))
        else:
            b3 = b.reshape(c // sub, sub, n)
            row = lax.broadcasted_iota(jnp.int32, b3.shape, 1)
            ref = jnp.broadcast_to(b3[:, half:half + 1], b3.shape)
            for m in range(blk, sub, blk):
                ref = jnp.where(row >= m, jnp.broadcast_to(b3[:, m + half:m + half + 1], b3.shape), ref)
            refs.append(ref.reshape(c, n))
    return refs


def _gla_rows(q_ref, k_ref, v_ref, g_ref, la_ref, tm_ref, mask_ref, sg_ref, on_ref, st_ref, o_ref):
    c = GLA_CHUNK
    hk = q_ref.shape[1] // GLA_HEADS
    hv = v_ref.shape[1] // GLA_HEADS
    qscale = float(hk) ** -0.5
    tmat = tm_ref[...]
    onorm = on_ref[...]

    for c0 in range(0, q_ref.shape[0], c):
        sl = slice(c0, c0 + c)
        la = la_ref[sl, :]
        h1 = la.astype(BF16)
        r1 = la - h1.astype(F32)
        h2 = r1.astype(BF16)
        h3 = (r1 - h2.astype(F32)).astype(BF16)
        ball = _dot(tmat, h1) + _dot(tmat, h2) + _dot(tmat, h3)
        refs = _level_refs(ball)

        heads = range(GLA_HEADS)
        ksl = [slice(h * hk, (h + 1) * hk) for h in heads]
        vsl = [slice(h * hv, (h + 1) * hv) for h in heads]
        b = [ball[:, ksl[h]] for h in heads]
        q = [q_ref[sl, ksl[h]].astype(F32) * qscale for h in heads]
        k = [k_ref[sl, ksl[h]].astype(F32) for h in heads]
        v = [v_ref[sl, vsl[h]] for h in heads]
        st = [st_ref[h] for h in heads]

        scores = []
        for h in heads:
            s = mask_ref[GLA_LEVELS] * lax.dot_general(
                q[h].astype(BF16), k[h].astype(BF16), _NT, preferred_element_type=F32)
            for lvl in range(GLA_LEVELS):
                ref = refs[lvl][:, ksl[h]]
                f = jnp.exp2((b[h] - ref) * sg_ref[lvl])
                s = s + mask_ref[lvl] * lax.dot_general(
                    (q[h] * f).astype(BF16), (k[h] * f).astype(BF16), _NT,
                    preferred_element_type=F32)
            scores.append(s.astype(BF16))

        outs = []
        for h in heads:
            o = lax.dot_general((q[h] * jnp.exp(b[h])).astype(BF16), st[h].astype(BF16), _NT,
                                preferred_element_type=F32)
            outs.append(o + _dot(scores[h], v[h]))

        for h in heads:
            blast = b[h][c - 1:c, :]
            kd = (k[h] * jnp.exp(blast - b[h])).astype(BF16)
            st_ref[h] = jnp.exp(blast) * st[h] + lax.dot_general(
                v[h], kd, _TN, preferred_element_type=F32)

        for h in heads:
            g = g_ref[sl, vsl[h]].astype(F32)
            y = _rms(outs[h], onorm) * (g * jax.nn.sigmoid(g))
            o_ref[sl, vsl[h]] = y.astype(o_ref.dtype)


def _gla_kernel(q_ref, k_ref, v_ref, g_ref, la_ref, tm_ref, mask_ref, sg_ref, on_ref, o_ref,
                st_ref):
    @pl.when(pl.program_id(1) == 0)
    def _():
        st_ref[...] = jnp.zeros_like(st_ref)

    _gla_rows(q_ref, k_ref, v_ref, g_ref, la_ref, tm_ref, mask_ref, sg_ref, on_ref, st_ref, o_ref)


def _gla_mix(proj, la, onorm, batch, seq, *, ts=GLA_ROWS):
    m = proj.shape[0]
    dk = la.shape[1]
    dv = (proj.shape[1] - 2 * dk) // 2
    hk = dk // GLA_HEADS
    hv = dv // GLA_HEADS
    nsb = seq // ts
    tmat, masks, signs = _gla_consts()
    rows = lambda width, col: pl.BlockSpec((ts, width), lambda b, s: (b * nsb + s, col))
    consts = (jnp.asarray(tmat, BF16), jnp.asarray(masks), jnp.asarray(signs), onorm)
    return pl.pallas_call(
        _gla_kernel,
        out_shape=jax.ShapeDtypeStruct((m, dv), BF16),
        grid=(batch, nsb),
        in_specs=[
            rows(dk, 0), rows(dk, 1), rows(dv, (2 * dk) // dv), rows(dv, (2 * dk) // dv + 1),
            rows(dk, 0),
        ] + [pl.BlockSpec(a.shape, lambda b, s, nd=a.ndim: (0,) * nd) for a in consts],
        out_specs=rows(dv, 0),
        scratch_shapes=[pltpu.VMEM((GLA_HEADS, hv, hk), F32)],
        compiler_params=pltpu.CompilerParams(dimension_semantics=("parallel", "arbitrary")),
        name="gla_mix",
    )(proj, proj, proj, proj, la, *consts)


def _sb_kernel(q_ref, k_ref, v_ref, o_ref, acc_ref, *, tq, hd):
    i = pl.program_id(2)
    nh = acc_ref.shape[0]
    lane = lax.broadcasted_iota(jnp.int32, (tq, LANES), 1)
    cols = [slice((h // 2) * LANES, (h // 2 + 1) * LANES) for h in range(nh)]
    qh = []
    for h in range(nh):
        q = q_ref[:, cols[h]]
        qh.append(jnp.where(lane < hd if h % 2 == 0 else lane >= hd, q, jnp.zeros_like(q)))
    half = tq // 2
    rr = lax.broadcasted_iota(jnp.int32, (tq, tq), 0)
    cc = lax.broadcasted_iota(jnp.int32, (tq, tq), 1)
    suffix = (rr >= cc).astype(BF16)
    causal_up = (lax.broadcasted_iota(jnp.int32, (half, half), 1)
                 < lax.broadcasted_iota(jnp.int32, (half, half), 0))
    causal_lo = (lax.broadcasted_iota(jnp.int32, (half, tq), 1)
                 < lax.broadcasted_iota(jnp.int32, (half, tq), 0) + half)

    def softplus_stage(h, rows, start, nk, mask):
        z = lax.dot_general(qh[h][rows], k_ref[pl.ds(start, nk), cols[h]], _NT,
                            preferred_element_type=F32)
        sp = jnp.maximum(z, jnp.log(1.0 + jnp.exp2(jnp.minimum(z, SB_EXP2_CAP))) * LOG2E)
        if mask is not None:
            sp = jnp.where(mask, sp, 0.0)
        return z, sp.astype(BF16), jnp.sum(sp, axis=-1, keepdims=True)

    def weight_stage(z, rem, carry, mask):
        w = jnp.exp2(z - rem - carry)
        if mask is not None:
            w = jnp.where(mask, w, 0.0)
        return w.astype(BF16)

    def tiles(groups, carries):
        flat = [(h, rows, *job) for h, rows, jobs in groups for job in jobs]
        first = [softplus_stage(h, rows, start, nk, mask) for h, rows, start, nk, mask, _ in flat]
        rems = [_dot(first[n][1], suffix[:flat[n][3], :flat[n][3]]) for n in range(len(flat))]
        mass = [m for _, _, m in first]
        ws = [weight_stage(first[n][0], rems[n], carries(n, mass), flat[n][4])
              for n in range(len(flat))]
        pvs, n = [], 0
        for h, _, jobs in groups:
            vs = []
            for start, nk, _, keep in jobs:
                v = v_ref[pl.ds(start, nk), cols[h]]
                vs.append(v if keep is None else jnp.where(keep, v, jnp.zeros_like(v)))
            pvs.append(_dot(jnp.concatenate(ws[n:n + len(jobs)], axis=1),
                            jnp.concatenate(vs, axis=0)))
            n += len(jobs)
        return pvs, mass

    diag = pl.multiple_of(i * tq, tq)
    prev = pl.multiple_of(jnp.maximum(i - 1, 0) * tq, tq)
    before = (prev, tq, None, i > 0)
    groups = []
    for h in range(nh):
        groups.append((h, slice(0, half), [(diag, half, causal_up, None), before]))
        groups.append((h, slice(half, tq), [(diag, tq, causal_lo, None), before]))
    pv, mass = tiles(groups, lambda n, mass: mass[n - 1] if n % 2 else jnp.zeros((half, 1), F32))
    carry = []
    for h in range(nh):
        acc_ref[h, 0:half] = pv[2 * h]
        acc_ref[h, half:tq] = pv[2 * h + 1]
        carry.append(jnp.concatenate([mass[4 * h] + mass[4 * h + 1],
                                      mass[4 * h + 2] + mass[4 * h + 3]], axis=0))

    def lowest(cs):
        return functools.reduce(jnp.minimum, [jnp.min(c) for c in cs])

    def live(c):
        t, _, low = c
        return jnp.logical_and(t < i - 1, low < SB_DEAD_MASS_LOG2)

    def body(c):
        t, cs, _ = c
        start = pl.multiple_of((i - 2 - t) * tq, tq)
        pv, mass = tiles([(h, slice(0, tq), [(start, tq, None, None)]) for h in range(nh)],
                         lambda n, mass: cs[n])
        for h in range(nh):
            acc_ref[h] += pv[h]
        cs = tuple(cs[h] + mass[h] for h in range(nh))
        return t + 1, cs, lowest(cs)

    lax.while_loop(live, body, (jnp.int32(0), tuple(carry), lowest(carry)))
    for p in range(nh // 2):
        o_ref[:, cols[2 * p]] = jnp.where(
            lane < hd, acc_ref[2 * p], acc_ref[2 * p + 1]).astype(o_ref.dtype)


def _sb_mix(qkv, batch, seq, *, tq=256, pairs=SB_PAIRS_PER_STEP):
    m = qkv.shape[0]
    d = qkv.shape[1] // 3
    hd = d // SB_HEADS
    ngrp = d // (pairs * LANES)
    nq = seq // tq
    width = pairs * LANES
    assert hd == SB_HEAD_DIM and d == PROJ_CHUNK
    kv_spec = lambda col0: pl.BlockSpec((seq, width), lambda b, p, i: (b, col0 + p))
    return pl.pallas_call(
        functools.partial(_sb_kernel, tq=tq, hd=hd),
        out_shape=jax.ShapeDtypeStruct((m, d), BF16),
        grid=(batch, ngrp, nq),
        in_specs=[
            pl.BlockSpec((tq, width), lambda b, p, i: (b * nq + i, p)),
            kv_spec(ngrp),
            kv_spec(2 * ngrp),
        ],
        out_specs=pl.BlockSpec((tq, width), lambda b, p, i: (b * nq + i, p)),
        scratch_shapes=[pltpu.VMEM((2 * pairs, tq, LANES), F32)],
        compiler_params=pltpu.CompilerParams(
            dimension_semantics=("parallel", "parallel", "arbitrary")),
        name="sb_mix",
    )(qkv, qkv, qkv)


def kernel(x, ffn1_norm, ffn1_w_gate, ffn1_w_up, ffn1_w_down, mix_norm, ffn2_norm, ffn2_w_gate, ffn2_w_up, ffn2_w_down, gla_w_in, gla_w_gk2, gla_b_gk, gla_o_norm, gla_w_out, sb_w_in, sb_w_out, final_norm):
    batch, seq, d = x.shape
    depth = ffn1_norm.shape[0]
    dk = gla_w_gk2.shape[2]
    n_main = gla_w_in.shape[2] - GLA_GATE_RANK
    xs = x.reshape(batch * seq, d)
    fg = final_norm.reshape(1, d)
    ffn1_w = tuple(w[0].astype(BF16) for w in (ffn1_w_gate, ffn1_w_up, ffn1_w_down))
    gla_in = gla_w_in.astype(BF16)
    mix_in = gla_in[0]
    for i in range(depth):
        j = i // 2
        gla = i % 2 == 0
        mg = mix_norm[i].reshape(1, d)
        ffn1 = (ffn1_norm[i].reshape(1, d), *ffn1_w)
        riders = [(ffn2_w_gate, i), (ffn2_w_up, i), (ffn2_w_down, i),
                  (gla_w_out if gla else sb_w_out, j)]
        if gla:
            wlr = jnp.pad(mix_in[:, n_main:], ((0, 0), (0, LANES - GLA_GATE_RANK)))
            wgk = jnp.pad(gla_w_gk2[j], ((0, LANES - GLA_GATE_RANK), (0, 0))).astype(BF16)
            (xs, proj, la), cast = _ffn_in(
                xs, ffn1, mg, (mix_in[:, :n_main], wlr, wgk, gla_b_gk[j].reshape(1, dk)), riders)
            y = _gla_mix(proj, la, gla_o_norm[j].reshape(1, -1), batch, seq)
        else:
            (xs, qkv), cast = _ffn_in(xs, ffn1, mg, (mix_in,), riders)
            y = _sb_mix(qkv, batch, seq)
        ffn2 = (ffn2_norm[i].reshape(1, d), *cast[:3])
        riders = []
        if i + 1 < depth:
            riders = [(ffn1_w_gate, i + 1), (ffn1_w_up, i + 1), (ffn1_w_down, i + 1)]
            if gla:
                riders.append((sb_w_in, (i + 1) // 2))
        xs, cast_next = _ffn_out(xs, y, cast[3], ffn2, fg, riders, final=(i == depth - 1))
        if riders:
            ffn1_w = cast_next[:3]
            mix_in = cast_next[3] if gla else gla_in[(i + 1) // 2]
    return xs.reshape(batch, seq, d)
```

```python
import functools

import numpy as np
import jax
import jax.numpy as jnp
from jax import lax
from jax.experimental import pallas as pl
from jax.experimental.pallas import tpu as pltpu

F32 = jnp.float32
BF16 = jnp.bfloat16

EPS = 1e-6
GLA_HEADS = 4
GLA_GATE_RANK = 16
GLA_GATE_TAU = 16.0
GLA_CHUNK = 64
GLA_LEVELS = 6
GLA_ROWS = 512
SB_HEADS = 16
SB_EXP2_CAP = 64.0
SB_PAIRS_PER_STEP = 4
LOG2E = 1.4426950408889634
SB_HEAD_DIM = 64
SB_ZSCALE = SB_HEAD_DIM ** -0.5 * LOG2E
SB_DEAD_MASS_LOG2 = 152.0
LANES = 128
BF16_SUBLANES = 16
MXU_DIM = 256
FFN_ROWS = 512
FFN_CHUNK = MXU_DIM
PROJ_CHUNK = 4 * MXU_DIM

_NT = (((1,), (1,)), ((), ()))
_TN = (((0,), (0,)), ((), ()))


def _rms(x, g):
    ms = jnp.mean(x * x, axis=-1, keepdims=True)
    return x * lax.rsqrt(ms + EPS) * g


def _dot(a, b):
    return jnp.dot(a, b, preferred_element_type=F32)


def _swiglu_half(x, g_ref, wg_ref, wu_ref, wd_ref):
    xn = _rms(x, g_ref[...]).astype(BF16)
    acc = None
    for f0 in range(0, wg_ref.shape[1], FFN_CHUNK):
        a = _dot(xn, wg_ref[:, f0:f0 + FFN_CHUNK])
        u = _dot(xn, wu_ref[:, f0:f0 + FFN_CHUNK])
        h = (a * jax.nn.sigmoid(a)) * u
        part = _dot(h.astype(BF16), wd_ref[f0:f0 + FFN_CHUNK, :])
        acc = part if acc is None else acc + part
    return x + 0.5 * acc


def _project_columns(xn, w_ref, o_ref, first_scale=None, between=()):
    between = list(between)
    for c0 in range(0, w_ref.shape[1], PROJ_CHUNK):
        p = _dot(xn, w_ref[:, c0:c0 + PROJ_CHUNK])
        if c0 == 0 and first_scale is not None:
            p = p * first_scale
        o_ref[:, c0:c0 + PROJ_CHUNK] = p.astype(o_ref.dtype)
        if between:
            between.pop(0)()


def _ffn_sb_kernel(x_ref, g_ref, wg_ref, wu_ref, wd_ref, mg_ref, w_ref, o_ref, p_ref):
    y = _swiglu_half(x_ref[...], g_ref, wg_ref, wu_ref, wd_ref)
    o_ref[...] = y
    _project_columns(_rms(y, mg_ref[...]).astype(BF16), w_ref, p_ref, first_scale=SB_ZSCALE)


def _ffn_gla_kernel(x_ref, g_ref, wg_ref, wu_ref, wd_ref, mg_ref, w_ref, wlr_ref, wgk_ref,
                    bgk_ref, o_ref, p_ref, la_ref):
    y = _swiglu_half(x_ref[...], g_ref, wg_ref, wu_ref, wd_ref)
    o_ref[...] = y
    hn = _rms(y, mg_ref[...]).astype(BF16)
    lr = _dot(hn, wlr_ref[...])
    stage = {}

    def gate_logits():
        stage["z"] = _dot(lr.astype(BF16), wgk_ref[...]) + bgk_ref[...]

    def gate_store():
        z = stage["z"]
        logsig = jnp.minimum(z, 0.0) - jnp.log(1.0 + jnp.exp(-jnp.abs(z)))
        la_ref[...] = logsig * (1.0 / GLA_GATE_TAU)

    _project_columns(hn, w_ref, p_ref, between=(gate_logits, gate_store))


def _ffn_out_kernel(x_ref, y_ref, wo_ref, g_ref, wg_ref, wu_ref, wd_ref, fg_ref, o_ref, *, final):
    x = x_ref[...] + _dot(y_ref[...], wo_ref[...])
    y = _swiglu_half(x, g_ref, wg_ref, wu_ref, wd_ref)
    if final:
        y = _rms(y, fg_ref[...])
    o_ref[...] = y


def _resident(a):
    return pl.BlockSpec(a.shape, lambda i: (0,) * a.ndim, pipeline_mode=pl.Buffered(1))


def _rows(tm, n):
    return pl.BlockSpec((tm, n), lambda i: (i, 0))


def _cast_rider_specs(riders, nsteps):
    in_specs, out_specs, out_shapes = [], [], []
    for src, layer in riders:
        _, r, c = src.shape
        nblk = max(n for n in (32, 16, 8, 4, 2, 1)
                   if nsteps % n == 0 and r % (n * BF16_SUBLANES) == 0)
        per = nsteps // nblk
        in_specs.append(pl.BlockSpec((None, r // nblk, c),
                                     lambda i, layer=layer, per=per: (layer, i // per, 0)))
        out_specs.append(pl.BlockSpec((r // nblk, c), lambda i, per=per: (i // per, 0)))
        out_shapes.append(jax.ShapeDtypeStruct((r, c), BF16))
    return in_specs, out_specs, out_shapes


def _with_cast_riders(body, n_in, n_rider):
    def kern(*refs):
        n_out = len(refs) - n_in - 2 * n_rider
        ins, rin = refs[:n_in], refs[n_in:n_in + n_rider]
        outs, rout = refs[n_in + n_rider:n_in + n_rider + n_out], refs[n_in + n_rider + n_out:]
        for src, dst in zip(rin, rout):
            dst[...] = src[...].astype(dst.dtype)
        body(*ins, *outs)
    return kern


def _ffn_call(body, name, x_like, consts, out_shapes, out_specs, riders, tm):
    m = x_like[0].shape[0]
    rin, rout, rshape = _cast_rider_specs(riders, m // tm)
    n_in = len(x_like) + len(consts)
    res = pl.pallas_call(
        _with_cast_riders(body, n_in, len(riders)),
        out_shape=tuple(out_shapes) + tuple(rshape),
        grid=(m // tm,),
        in_specs=[_rows(tm, a.shape[1]) for a in x_like] + [_resident(a) for a in consts] + rin,
        out_specs=tuple(out_specs) + tuple(rout),
        compiler_params=pltpu.CompilerParams(dimension_semantics=("arbitrary",)),
        name=name,
    )(*x_like, *consts, *[src for src, _ in riders])
    return res[:len(out_shapes)], res[len(out_shapes):]


def _ffn_in(x, ffn_w, mg, mix_w, riders, *, tm=FFN_ROWS):
    m, d = x.shape
    n = mix_w[0].shape[1]
    gla = len(mix_w) > 1
    out_shape = [jax.ShapeDtypeStruct((m, d), F32), jax.ShapeDtypeStruct((m, n), BF16)]
    out_specs = [_rows(tm, d), _rows(tm, n)]
    if gla:
        dk = mix_w[2].shape[1]
        out_shape.append(jax.ShapeDtypeStruct((m, dk), F32))
        out_specs.append(_rows(tm, dk))
    return _ffn_call(_ffn_gla_kernel if gla else _ffn_sb_kernel,
                     "ffn_gla_proj" if gla else "ffn_sb_proj",
                     (x,), (*ffn_w, mg, *mix_w), out_shape, out_specs, riders, tm)


def _ffn_out(x, y, wo, ffn_w, fg, riders, *, final, tm=2 * FFN_ROWS):
    m, d = x.shape
    (out,), cast = _ffn_call(functools.partial(_ffn_out_kernel, final=final), "out_proj_ffn",
                             (x, y), (wo, *ffn_w, fg), [jax.ShapeDtypeStruct((m, d), F32)],
                             [_rows(tm, d)], riders, tm)
    return out, cast


def _gla_consts():
    c = GLA_CHUNK
    r = np.arange(c)
    tmat = r[None, :] <= r[:, None]
    masks, signs = [], []
    for lvl in range(GLA_LEVELS):
        n = c >> lvl
        half = n // 2
        same = (r[:, None] // n) == (r[None, :] // n)
        masks.append(same & ((r[:, None] % n) >= half) & ((r[None, :] % n) < half))
        signs.append(np.where((r % n) >= half, LOG2E, -LOG2E))
    masks.append(r[:, None] == r[None, :])
    signs = np.broadcast_to(np.stack(signs, 0)[:, :, None], (GLA_LEVELS, c, LANES))
    return (tmat.astype(np.float32), np.stack(masks, 0).astype(np.float32),
            np.ascontiguousarray(signs, dtype=np.float32))


def _level_refs(b):
    c, n = b.shape
    sub = 8
    refs = []
    for lvl in range(GLA_LEVELS):
        blk = c >> lvl
        half = blk // 2
        if blk >= 2 * sub:
            refs.append(jnp.concatenate(
                [jnp.broadcast_to(b[m + half:m + half + 1], (blk, n)) for m in range(0, c, blk)], 0))
        else:
            b3 = b.reshape(c // sub, sub, n)
            row = lax.broadcasted_iota(jnp.int32, b3.shape, 1)
            ref = jnp.broadcast_to(b3[:, half:half + 1], b3.shape)
            for m in range(blk, sub, blk):
                ref = jnp.where(row >= m, jnp.broadcast_to(b3[:, m + half:m + half + 1], b3.shape), ref)
            refs.append(ref.reshape(c, n))
    return refs


def _gla_rows(q_ref, k_ref, v_ref, g_ref, la_ref, tm_ref, mask_ref, sg_ref, on_ref, st_ref, o_ref):
    c = GLA_CHUNK
    hk = q_ref.shape[1] // GLA_HEADS
    hv = v_ref.shape[1] // GLA_HEADS
    qscale = float(hk) ** -0.5
    tmat = tm_ref[...]
    onorm = on_ref[...]

    for c0 in range(0, q_ref.shape[0], c):
        sl = slice(c0, c0 + c)
        la = la_ref[sl, :]
        h1 = la.astype(BF16)
        r1 = la - h1.astype(F32)
        h2 = r1.astype(BF16)
        h3 = (r1 - h2.astype(F32)).astype(BF16)
        ball = _dot(tmat, h1) + _dot(tmat, h2) + _dot(tmat, h3)
        refs = _level_refs(ball)

        heads = range(GLA_HEADS)
        ksl = [slice(h * hk, (h + 1) * hk) for h in heads]
        vsl = [slice(h * hv, (h + 1) * hv) for h in heads]
        b = [ball[:, ksl[h]] for h in heads]
        q = [q_ref[sl, ksl[h]].astype(F32) * qscale for h in heads]
        k = [k_ref[sl, ksl[h]].astype(F32) for h in heads]
        v = [v_ref[sl, vsl[h]] for h in heads]
        st = [st_ref[h] for h in heads]

        scores = []
        for h in heads:
            s = mask_ref[GLA_LEVELS] * lax.dot_general(
                q[h].astype(BF16), k[h].astype(BF16), _NT, preferred_element_type=F32)
            for lvl in range(GLA_LEVELS):
                ref = refs[lvl][:, ksl[h]]
                f = jnp.exp2((b[h] - ref) * sg_ref[lvl])
                s = s + mask_ref[lvl] * lax.dot_general(
                    (q[h] * f).astype(BF16), (k[h] * f).astype(BF16), _NT,
                    preferred_element_type=F32)
            scores.append(s.astype(BF16))

        outs = []
        for h in heads:
            o = lax.dot_general((q[h] * jnp.exp(b[h])).astype(BF16), st[h].astype(BF16), _NT,
                                preferred_element_type=F32)
            outs.append(o + _dot(scores[h], v[h]))

        for h in heads:
            blast = b[h][c - 1:c, :]
            kd = (k[h] * jnp.exp(blast - b[h])).astype(BF16)
            st_ref[h] = jnp.exp(blast) * st[h] + lax.dot_general(
                v[h], kd, _TN, preferred_element_type=F32)

        for h in heads:
            g = g_ref[sl, vsl[h]].astype(F32)
            y = _rms(outs[h], onorm) * (g * jax.nn.sigmoid(g))
            o_ref[sl, vsl[h]] = y.astype(o_ref.dtype)


def _gla_kernel(q_ref, k_ref, v_ref, g_ref, la_ref, tm_ref, mask_ref, sg_ref, on_ref, o_ref,
                st_ref):
    @pl.when(pl.program_id(1) == 0)
    def _():
        st_ref[...] = jnp.zeros_like(st_ref)

    _gla_rows(q_ref, k_ref, v_ref, g_ref, la_ref, tm_ref, mask_ref, sg_ref, on_ref, st_ref, o_ref)


def _gla_mix(proj, la, onorm, batch, seq, *, ts=GLA_ROWS):
    m = proj.shape[0]
    dk = la.shape[1]
    dv = (proj.shape[1] - 2 * dk) // 2
    hk = dk // GLA_HEADS
    hv = dv // GLA_HEADS
    nsb = seq // ts
    tmat, masks, signs = _gla_consts()
    rows = lambda width, col: pl.BlockSpec((ts, width), lambda b, s: (b * nsb + s, col))
    consts = (jnp.asarray(tmat, BF16), jnp.asarray(masks), jnp.asarray(signs), onorm)
    return pl.pallas_call(
        _gla_kernel,
        out_shape=jax.ShapeDtypeStruct((m, dv), BF16),
        grid=(batch, nsb),
        in_specs=[
            rows(dk, 0), rows(dk, 1), rows(dv, (2 * dk) // dv), rows(dv, (2 * dk) // dv + 1),
            rows(dk, 0),
        ] + [pl.BlockSpec(a.shape, lambda b, s, nd=a.ndim: (0,) * nd) for a in consts],
        out_specs=rows(dv, 0),
        scratch_shapes=[pltpu.VMEM((GLA_HEADS, hv, hk), F32)],
        compiler_params=pltpu.CompilerParams(dimension_semantics=("parallel", "arbitrary")),
        name="gla_mix",
    )(proj, proj, proj, proj, la, *consts)


def _sb_kernel(q_ref, k_ref, v_ref, o_ref, acc_ref, *, tq, hd):
    i = pl.program_id(2)
    nh = acc_ref.shape[0]
    lane = lax.broadcasted_iota(jnp.int32, (tq, LANES), 1)
    cols = [slice((h // 2) * LANES, (h // 2 + 1) * LANES) for h in range(nh)]
    qh = []
    for h in range(nh):
        q = q_ref[:, cols[h]]
        qh.append(jnp.where(lane < hd if h % 2 == 0 else lane >= hd, q, jnp.zeros_like(q)))
    half = tq // 2
    rr = lax.broadcasted_iota(jnp.int32, (tq, tq), 0)
    cc = lax.broadcasted_iota(jnp.int32, (tq, tq), 1)
    suffix = (rr > cc).astype(BF16)
    causal_up = (lax.broadcasted_iota(jnp.int32, (half, half), 1)
                 < lax.broadcasted_iota(jnp.int32, (half, half), 0))
    causal_lo = (lax.broadcasted_iota(jnp.int32, (half, tq), 1)
                 < lax.broadcasted_iota(jnp.int32, (half, tq), 0) + half)

    def softplus_stage(h, rows, start, nk, mask):
        z = lax.dot_general(qh[h][rows], k_ref[pl.ds(start, nk), cols[h]], _NT,
                            preferred_element_type=F32)
        sp = jnp.maximum(z, jnp.log(1.0 + jnp.exp2(jnp.minimum(z, SB_EXP2_CAP))) * LOG2E)
        logbeta = z - sp
        if mask is not None:
            sp = jnp.where(mask, sp, 0.0)
        return logbeta, sp.astype(BF16), jnp.sum(sp, axis=-1, keepdims=True)

    def weight_stage(logbeta, rem, carry, mask):
        w = jnp.exp2(logbeta - rem - carry)
        if mask is not None:
            w = jnp.where(mask, w, 0.0)
        return w.astype(BF16)

    def tiles(groups, carries):
        flat = [(h, rows, *job) for h, rows, jobs in groups for job in jobs]
        first = [softplus_stage(h, rows, start, nk, mask) for h, rows, start, nk, mask, _ in flat]
        rems = [_dot(first[n][1], suffix[:flat[n][3], :flat[n][3]]) for n in range(len(flat))]
        mass = [m for _, _, m in first]
        ws = [weight_stage(first[n][0], rems[n], carries(n, mass), flat[n][4])
              for n in range(len(flat))]
        pvs, n = [], 0
        for h, _, jobs in groups:
            vs = []
            for start, nk, _, keep in jobs:
                v = v_ref[pl.ds(start, nk), cols[h]]
                vs.append(v if keep is None else jnp.where(keep, v, jnp.zeros_like(v)))
            pvs.append(_dot(jnp.concatenate(ws[n:n + len(jobs)], axis=1),
                            jnp.concatenate(vs, axis=0)))
            n += len(jobs)
        return pvs, mass

    diag = pl.multiple_of(i * tq, tq)
    prev = pl.multiple_of(jnp.maximum(i - 1, 0) * tq, tq)
    before = (prev, tq, None, i > 0)
    groups = []
    for h in range(nh):
        groups.append((h, slice(0, half), [(diag, half, causal_up, None), before]))
        groups.append((h, slice(half, tq), [(diag, tq, causal_lo, None), before]))
    pv, mass = tiles(groups, lambda n, mass: mass[n - 1] if n % 2 else jnp.zeros((half, 1), F32))
    carry = []
    for h in range(nh):
        acc_ref[h, 0:half] = pv[2 * h]
        acc_ref[h, half:tq] = pv[2 * h + 1]
        carry.append(jnp.concatenate([mass[4 * h] + mass[4 * h + 1],
                                      mass[4 * h + 2] + mass[4 * h + 3]], axis=0))

    def lowest(cs):
        return functools.reduce(jnp.minimum, [jnp.min(c) for c in cs])

    def live(c):
        t, _, low = c
        return jnp.logical_and(t < i - 1, low < SB_DEAD_MASS_LOG2)

    def body(c):
        t, cs, _ = c
        start = pl.multiple_of((i - 2 - t) * tq, tq)
        pv, mass = tiles([(h, slice(0, tq), [(start, tq, None, None)]) for h in range(nh)],
                         lambda n, mass: cs[n])
        for h in range(nh):
            acc_ref[h] += pv[h]
        cs = tuple(cs[h] + mass[h] for h in range(nh))
        return t + 1, cs, lowest(cs)

    lax.while_loop(live, body, (jnp.int32(0), tuple(carry), lowest(carry)))
    for p in range(nh // 2):
        o_ref[:, cols[2 * p]] = jnp.where(
            lane < hd, acc_ref[2 * p], acc_ref[2 * p + 1]).astype(o_ref.dtype)


def _sb_mix(qkv, batch, seq, *, tq=256, pairs=SB_PAIRS_PER_STEP):
    m = qkv.shape[0]
    d = qkv.shape[1] // 3
    hd = d // SB_HEADS
    ngrp = d // (pairs * LANES)
    nq = seq // tq
    width = pairs * LANES
    assert hd == SB_HEAD_DIM and d == PROJ_CHUNK
    kv_spec = lambda col0: pl.BlockSpec((seq, width), lambda b, p, i: (b, col0 + p))
    return pl.pallas_call(
        functools.partial(_sb_kernel, tq=tq, hd=hd),
        out_shape=jax.ShapeDtypeStruct((m, d), BF16),
        grid=(batch, ngrp, nq),
        in_specs=[
            pl.BlockSpec((tq, width), lambda b, p, i: (b * nq + i, p)),
            kv_spec(ngrp),
            kv_spec(2 * ngrp),
        ],
        out_specs=pl.BlockSpec((tq, width), lambda b, p, i: (b * nq + i, p)),
        scratch_shapes=[pltpu.VMEM((2 * pairs, tq, LANES), F32)],
        compiler_params=pltpu.CompilerParams(
            dimension_semantics=("parallel", "parallel", "arbitrary")),
        name="sb_mix",
    )(qkv, qkv, qkv)


def kernel(x, ffn1_norm, ffn1_w_gate, ffn1_w_up, ffn1_w_down, mix_norm, ffn2_norm, ffn2_w_gate, ffn2_w_up, ffn2_w_down, gla_w_in, gla_w_gk2, gla_b_gk, gla_o_norm, gla_w_out, sb_w_in, sb_w_out, final_norm):
    batch, seq, d = x.shape
    depth = ffn1_norm.shape[0]
    dk = gla_w_gk2.shape[2]
    n_main = gla_w_in.shape[2] - GLA_GATE_RANK
    xs = x.reshape(batch * seq, d)
    fg = final_norm.reshape(1, d)
    ffn1_w = tuple(w[0].astype(BF16) for w in (ffn1_w_gate, ffn1_w_up, ffn1_w_down))
    gla_in = gla_w_in.astype(BF16)
    mix_in = gla_in[0]
    for i in range(depth):
        j = i // 2
        gla = i % 2 == 0
        mg = mix_norm[i].reshape(1, d)
        ffn1 = (ffn1_norm[i].reshape(1, d), *ffn1_w)
        riders = [(ffn2_w_gate, i), (ffn2_w_up, i), (ffn2_w_down, i),
                  (gla_w_out if gla else sb_w_out, j)]
        if gla:
            wlr = jnp.pad(mix_in[:, n_main:], ((0, 0), (0, LANES - GLA_GATE_RANK)))
            wgk = jnp.pad(gla_w_gk2[j], ((0, LANES - GLA_GATE_RANK), (0, 0))).astype(BF16)
            (xs, proj, la), cast = _ffn_in(
                xs, ffn1, mg, (mix_in[:, :n_main], wlr, wgk, gla_b_gk[j].reshape(1, dk)), riders)
            y = _gla_mix(proj, la, gla_o_norm[j].reshape(1, -1), batch, seq)
        else:
            (xs, qkv), cast = _ffn_in(xs, ffn1, mg, (mix_in,), riders)
            y = _sb_mix(qkv, batch, seq)
        ffn2 = (ffn2_norm[i].reshape(1, d), *cast[:3])
        riders = []
        if i + 1 < depth:
            riders = [(ffn1_w_gate, i + 1), (ffn1_w_up, i + 1), (ffn1_w_down, i + 1)]
            if gla:
                riders.append((sb_w_in, (i + 1) // 2))
        xs, cast_next = _ffn_out(xs, y, cast[3], ffn2, fg, riders, final=(i == depth - 1))
        if riders:
            ffn1_w = cast_next[:3]
            mix_in = cast_next[3] if gla else gla_in[(i + 1) // 2]
    return xs.reshape(batch, seq, d)
```
